```python
import math
import jax, jax.numpy as jnp
from jax import lax
import numpy as np

D_MODEL = 1024
BATCH = 2
SEQ = 8192
DEPTH = 4
DEC_BATCH = 32
DEC_SEQ = 4
PAST_LEN = 8192
PAGE_SIZE = 128

EPS = 1e-6
GDN_HEADS = 4
GDN_DK = 128
GDN_DV = 128
GDN_CONV = 4
GDN_CHUNK = 64
GDN_CONV_CH = 2 * GDN_HEADS * GDN_DK + GDN_HEADS * GDN_DV
DIFF_HEADS = 4
DIFF_DH = 64
DIFF_WIDTH = DIFF_HEADS * 2 * DIFF_DH
ROPE_THETA = 500000.0
ROPE_DIM = DIFF_DH // 4
Q_BLOCK = 128
GMLP_GROUPS = 4
GMLP_DG = 128
GMLP_CHUNK = 128
GMLP_WIDTH = GMLP_GROUPS * GMLP_DG
MEM_LEN = 256
XA_HEADS = 4
XA_DH = 128
XA_WIDTH = XA_HEADS * XA_DH
D_FF = 4 * D_MODEL
N_BRANCH = 3

SPLIT_SIZES = (
    GDN_HEADS * GDN_DK, GDN_HEADS * GDN_DK, GDN_HEADS * GDN_DV,
    GDN_HEADS, GDN_HEADS,
    GDN_HEADS * GDN_DV,
    DIFF_WIDTH, DIFF_WIDTH, DIFF_WIDTH,
    GMLP_WIDTH, GMLP_WIDTH,
    N_BRANCH * D_MODEL,
)
SPLIT_IDX = [int(i) for i in np.cumsum(SPLIT_SIZES)[:-1]]
D_IN = int(sum(SPLIT_SIZES))

kernel_name = 'hybrid_gdn_diffattn_gmlp_step'


def rmsnorm(x, g):
    xf = x.astype(jnp.float32)
    y = xf * lax.rsqrt(jnp.mean(xf * xf, axis=-1, keepdims=True) + EPS)
    return (y * g.astype(jnp.float32)).astype(x.dtype)


def layernorm(x, g, b):
    xf = x.astype(jnp.float32)
    mu = jnp.mean(xf, axis=-1, keepdims=True)
    xc = xf - mu
    y = xc * lax.rsqrt(jnp.mean(xc * xc, axis=-1, keepdims=True) + EPS)
    return (y * g.astype(jnp.float32) + b.astype(jnp.float32)).astype(x.dtype)


def l2norm(x):
    return x * lax.rsqrt(jnp.sum(x * x, axis=-1, keepdims=True) + EPS)


def rope(x, pos):
    half = ROPE_DIM // 2
    inv = ROPE_THETA ** (-jnp.arange(0, ROPE_DIM, 2, dtype=jnp.float32) / ROPE_DIM)
    ang = pos.astype(jnp.float32)[:, None] * inv[None, :]
    cos = jnp.cos(ang)[None, :, None, None, :]
    sin = jnp.sin(ang)[None, :, None, None, :]
    xr = x[..., :ROPE_DIM].astype(jnp.float32)
    x1, x2 = xr[..., :half], xr[..., half:]
    rot = jnp.concatenate([x1 * cos - x2 * sin, x2 * cos + x1 * sin], axis=-1)
    return jnp.concatenate([rot.astype(x.dtype), x[..., ROPE_DIM:]], axis=-1)


def short_conv(x, buf, w):
    L = x.shape[1]
    xp = jnp.concatenate([buf, x], axis=1)
    y = xp[:, 0:L] * w[0]
    for i in range(1, GDN_CONV):
        y = y + xp[:, i:i + L] * w[i]
    return y, xp[:, L:]


def gdn_chunked(q, k, v, g, beta, S0):
    B, L, H, DK = q.shape
    DV = v.shape[-1]
    C = min(GDN_CHUNK, L)
    pad = (-L) % C
    if pad:
        p4 = ((0, 0), (0, pad), (0, 0), (0, 0))
        p3 = ((0, 0), (0, pad), (0, 0))
        q, k, v = jnp.pad(q, p4), jnp.pad(k, p4), jnp.pad(v, p4)
        g, beta = jnp.pad(g, p3), jnp.pad(beta, p3)
    N = (L + pad) // C
    blk = lambda t: jnp.moveaxis(t.reshape(B, N, C, *t.shape[2:]), 3, 2)
    qc, kc, vc, gc, bc = blk(q), blk(k), blk(v), blk(g), blk(beta)
    G = jnp.cumsum(gc, axis=-1)
    diff = G[..., :, None] - G[..., None, :]
    idx = jnp.arange(C)
    incl = idx[:, None] >= idx[None, :]
    strict = idx[:, None] > idx[None, :]
    dec_incl = jnp.exp(jnp.where(incl, diff, -jnp.inf))
    dec_strict = jnp.where(strict, dec_incl, 0.0)
    A = bc[..., :, None] * jnp.einsum('bnhid,bnhjd->bnhij', kc, kc) * dec_strict
    Gexp = jnp.exp(G)
    rhs = jnp.concatenate([bc[..., None] * vc, (bc * Gexp)[..., None] * kc], axis=-1)
    X = lax.linalg.triangular_solve(A + jnp.eye(C, dtype=A.dtype), rhs, left_side=True,
                                    lower=True, unit_diagonal=True)
    U0, W = X[..., :DV], X[..., DV:]
    P = jnp.einsum('bnhid,bnhjd->bnhij', qc, kc) * dec_incl
    tail = jnp.exp(G[..., -1:] - G)
    gend = Gexp[..., -1]
    xs = tuple(jnp.moveaxis(t, 1, 0) for t in (U0, W, P, qc, kc, Gexp, tail, gend))

    def step(S, inp):
        u0, w, p, qq, kk, ge, tl, gn = inp
        U = u0 - jnp.einsum('bhck,bhvk->bhcv', w, S)
        O = ge[..., None] * jnp.einsum('bhck,bhvk->bhcv', qq, S) + jnp.einsum('bhij,bhjv->bhiv', p, U)
        S = gn[..., None, None] * S + jnp.einsum('bhcv,bhck->bhvk', U * tl[..., None], kk)
        return S, O

    S, O = lax.scan(step, S0, xs)
    O = jnp.moveaxis(jnp.moveaxis(O, 0, 1), 2, 3).reshape(B, N * C, H, DV)[:, :L]
    return O, S


def gdn_mixer(q_in, k_in, v_in, a_in, b_in, g_in, conv_buf, S0, conv_w, A_log, dt_bias, o_norm):
    B, L, _ = q_in.shape
    dt = q_in.dtype
    qkv = jnp.concatenate([q_in, k_in, v_in], axis=-1)
    y, new_buf = short_conv(qkv, conv_buf.astype(dt), conv_w)
    y = jax.nn.silu(y).astype(jnp.float32)
    q, k, v = jnp.split(y, [GDN_HEADS * GDN_DK, 2 * GDN_HEADS * GDN_DK], axis=-1)
    q = l2norm(q.reshape(B, L, GDN_HEADS, GDN_DK)) * GDN_DK ** -0.5
    k = l2norm(k.reshape(B, L, GDN_HEADS, GDN_DK))
    v = v.reshape(B, L, GDN_HEADS, GDN_DV)
    g = -jnp.exp(A_log.astype(jnp.float32)) * jax.nn.softplus(a_in.astype(jnp.float32) + dt_bias.astype(jnp.float32))
    beta = jax.nn.sigmoid(b_in.astype(jnp.float32))
    o, S = gdn_chunked(q, k, v, g, beta, S0.astype(jnp.float32))
    o = rmsnorm(o, o_norm) * jax.nn.silu(g_in.reshape(B, L, GDN_HEADS, GDN_DV).astype(jnp.float32))
    return o.reshape(B, L, GDN_HEADS * GDN_DV).astype(dt), new_buf, S.astype(dt)


def diff_attend(q, k, v, q_pos, k_pos, lam):
    B, L = q.shape[:2]
    qb = min(Q_BLOCK, L)
    nb = L // qb
    qs = jnp.moveaxis(q.reshape(B, nb, qb, *q.shape[2:]), 1, 0)
    ps = q_pos.reshape(nb, qb)

    def one(args):
        qblk, pblk = args
        s = jnp.einsum('bqhcd,bkhcd->bhcqk', qblk, k).astype(jnp.float32) * DIFF_DH ** -0.5
        s = jnp.where(k_pos[None, :] <= pblk[:, None], s, -jnp.inf)
        p = jax.nn.softmax(s, axis=-1)
        pd = p[:, :, 0] - lam * p[:, :, 1]
        return jnp.einsum('bhqk,bkhe->bqhe', pd.astype(v.dtype), v)

    o = lax.map(one, (qs, ps))
    return jnp.moveaxis(o, 0, 1).reshape(B, L, *o.shape[3:])


def gmlp_mixer(u, v, ln_g, ln_b, ws, bs):
    u = jax.nn.gelu(u)
    v = layernorm(jax.nn.gelu(v), ln_g, ln_b)
    B, L, _ = v.shape
    pad = (-L) % GMLP_CHUNK
    vp = jnp.pad(v, ((0, 0), (0, pad), (0, 0)))
    N = (L + pad) // GMLP_CHUNK
    vc = vp.reshape(B, N, GMLP_CHUNK, GMLP_GROUPS, GMLP_DG)
    tri = jnp.tril(jnp.ones((GMLP_CHUNK, GMLP_CHUNK), dtype=bool))
    w = jnp.where(tri[None], ws, jnp.zeros_like(ws))
    mixed = jnp.einsum('gts,bnsgc->bntgc', w, vc) + jnp.transpose(bs)[None, None, :, :, None]
    mixed = mixed.reshape(B, N * GMLP_CHUNK, GMLP_WIDTH)[:, :L]
    return u * mixed, v


def mem_cross_attn(h, mk, mv, wq, wo):
    B, L, _ = h.shape
    q = (h @ wq).reshape(B, L, XA_HEADS, XA_DH)
    s = jnp.einsum('blhd,bmhd->bhlm', q, mk).astype(jnp.float32) * XA_DH ** -0.5
    p = jax.nn.softmax(s, axis=-1)
    o = jnp.einsum('bhlm,bmhd->blhd', p.astype(mv.dtype), mv)
    return o.reshape(B, L, XA_WIDTH) @ wo


def setup_inputs(seed: int = 0) -> dict:
    key = jax.random.key(seed)
    ks = iter(jax.random.split(key, 64))
    f32 = jnp.float32

    def nrm(shape, scale=1.0):
        return jax.random.normal(next(ks), shape, f32) * scale

    def gain(shape):
        return 1.0 + 0.01 * jax.random.normal(next(ks), shape, f32)

    n_pages = PAST_LEN // PAGE_SIZE
    n_used = DEC_BATCH * n_pages
    n_pool = n_used + max(1, n_used // 4)
    page_table = jax.random.permutation(next(ks), n_pool)[:n_used].reshape(DEC_BATCH, n_pages).astype(jnp.int32)
    dt = jnp.exp(jax.random.uniform(next(ks), (DEPTH, GDN_HEADS), f32, math.log(1e-3), math.log(1e-1)))
    return {
        'x_prompt': nrm((BATCH, SEQ, D_MODEL)),
        'x_sample': nrm((DEC_BATCH, DEC_SEQ, D_MODEL)),
        'cache_diff_k': nrm((DEPTH, n_pool, PAGE_SIZE, DIFF_HEADS, 2 * DIFF_DH)),
        'cache_diff_v': nrm((DEPTH, n_pool, PAGE_SIZE, DIFF_HEADS, 2 * DIFF_DH)),
        'page_table': page_table,
        'cache_mem_k': nrm((DEPTH, DEC_BATCH, MEM_LEN, XA_HEADS, XA_DH)),
        'cache_mem_v': nrm((DEPTH, DEC_BATCH, MEM_LEN, XA_HEADS, XA_DH)),
        'state_gdn': nrm((DEPTH, DEC_BATCH, GDN_HEADS, GDN_DV, GDN_DK), 0.5),
        'state_gdn_conv': nrm((DEPTH, DEC_BATCH, GDN_CONV - 1, GDN_CONV_CH)),
        'mem_prompt': nrm((BATCH, MEM_LEN, D_MODEL)),
        'w_in': nrm((DEPTH, D_MODEL, D_IN), D_MODEL ** -0.5),
        'norm_mix': gain((DEPTH, D_MODEL)),
        'gdn_conv_w': nrm((DEPTH, GDN_CONV, GDN_CONV_CH), 0.5),
        'gdn_A_log': jnp.log(jax.random.uniform(next(ks), (DEPTH, GDN_HEADS), f32, 1.0, 16.0)),
        'gdn_dt_bias': dt + jnp.log(-jnp.expm1(-dt)),
        'gdn_o_norm': gain((DEPTH, GDN_DV)),
        'diff_lq1': nrm((DEPTH, DIFF_DH), 0.1),
        'diff_lk1': nrm((DEPTH, DIFF_DH), 0.1),
        'diff_lq2': nrm((DEPTH, DIFF_DH), 0.1),
        'diff_lk2': nrm((DEPTH, DIFF_DH), 0.1),
        'diff_subln': gain((DEPTH, 2 * DIFF_DH)),
        'gmlp_ln_g': gain((DEPTH, GMLP_WIDTH)),
        'gmlp_ln_b': nrm((DEPTH, GMLP_WIDTH), 0.01),
        'gmlp_ws': nrm((DEPTH, GMLP_GROUPS, GMLP_CHUNK, GMLP_CHUNK), GMLP_CHUNK ** -0.5),
        'gmlp_b': 1.0 + nrm((DEPTH, GMLP_GROUPS, GMLP_CHUNK), 0.1),
        'w_br_gdn': nrm((DEPTH, GDN_HEADS * GDN_DV, D_MODEL), (GDN_HEADS * GDN_DV) ** -0.5),
        'w_br_diff': nrm((DEPTH, DIFF_WIDTH, D_MODEL), DIFF_WIDTH ** -0.5),
        'w_br_gmlp': nrm((DEPTH, GMLP_WIDTH, D_MODEL), GMLP_WIDTH ** -0.5),
        'w_out': nrm((DEPTH, D_MODEL, D_MODEL), D_MODEL ** -0.5),
        'norm_xa': gain((DEPTH, D_MODEL)),
        'norm_mem': gain((DEPTH, D_MODEL)),
        'xa_wq': nrm((DEPTH, D_MODEL, XA_WIDTH), D_MODEL ** -0.5),
        'xa_wk': nrm((DEPTH, D_MODEL, XA_WIDTH), D_MODEL ** -0.5),
        'xa_wv': nrm((DEPTH, D_MODEL, XA_WIDTH), D_MODEL ** -0.5),
        'xa_wo': nrm((DEPTH, XA_WIDTH, D_MODEL), XA_WIDTH ** -0.5),
        'norm_ffn': gain((DEPTH, D_MODEL)),
        'ffn_w1': nrm((DEPTH, D_MODEL, D_FF), D_MODEL ** -0.5),
        'ffn_w2': nrm((DEPTH, D_FF, D_MODEL), D_FF ** -0.5),
        'norm_final': gain((D_MODEL,)),
    }


def reference(x_prompt, x_sample, cache_diff_k, cache_diff_v, page_table, cache_mem_k, cache_mem_v,
              state_gdn, state_gdn_conv, mem_prompt, w_in, norm_mix, gdn_conv_w, gdn_A_log, gdn_dt_bias,
              gdn_o_norm, diff_lq1, diff_lk1, diff_lq2, diff_lk2, diff_subln, gmlp_ln_g, gmlp_ln_b, gmlp_ws,
              gmlp_b, w_br_gdn, w_br_diff, w_br_gmlp, w_out, norm_xa, norm_mem, xa_wq, xa_wk, xa_wv, xa_wo,
              norm_ffn, ffn_w1, ffn_w2, norm_final):
    f32 = jnp.float32

    def trunk(x, q_pos, mem_k, mem_v, S_in, buf_in, paged):
        B, L, _ = x.shape
        new_k, new_v, new_S, new_buf, new_gv = [], [], [], [], []
        for l in range(DEPTH):
            h = rmsnorm(x, norm_mix[l])
            (a_q, a_k, a_v, a_a, a_b, a_g, b_q, b_k, b_v, c_u, c_v, gates) = jnp.split(h @ w_in[l], SPLIT_IDX, axis=-1)
            o_a, buf, S = gdn_mixer(a_q, a_k, a_v, a_a, a_b, a_g, buf_in[l], S_in[l], gdn_conv_w[l],
                                    gdn_A_log[l], gdn_dt_bias[l], gdn_o_norm[l])
            q = rope(b_q.reshape(B, L, DIFF_HEADS, 2, DIFF_DH), q_pos)
            k = rope(b_k.reshape(B, L, DIFF_HEADS, 2, DIFF_DH), q_pos)
            v = b_v.reshape(B, L, DIFF_HEADS, 2 * DIFF_DH)
            if paged is None:
                k_all, v_all, k_pos = k, v, q_pos
            else:
                ck, cv, pt = paged
                pk = ck[l][pt].reshape(B, PAST_LEN, DIFF_HEADS, 2, DIFF_DH).astype(k.dtype)
                pv = cv[l][pt].reshape(B, PAST_LEN, DIFF_HEADS, 2 * DIFF_DH).astype(v.dtype)
                k_all = jnp.concatenate([pk, k], axis=1)
                v_all = jnp.concatenate([pv, v], axis=1)
                k_pos = jnp.arange(PAST_LEN + L)
            lam_init = 0.8 - 0.6 * math.exp(-0.3 * l)
            lam = (jnp.exp(jnp.sum(diff_lq1[l].astype(f32) * diff_lk1[l].astype(f32)))
                   - jnp.exp(jnp.sum(diff_lq2[l].astype(f32) * diff_lk2[l].astype(f32))) + lam_init)
            o_b = diff_attend(q, k_all, v_all, q_pos, k_pos, lam)
            o_b = (rmsnorm(o_b, diff_subln[l]) * (1.0 - lam_init)).reshape(B, L, DIFF_WIDTH)
            o_c, gv = gmlp_mixer(c_u, c_v, gmlp_ln_g[l], gmlp_ln_b[l], gmlp_ws[l], gmlp_b[l])
            gt = jax.nn.sigmoid(gates.reshape(B, L, N_BRANCH, D_MODEL).astype(f32)).astype(x.dtype)
            merged = (gt[:, :, 0] * (o_a @ w_br_gdn[l]) + gt[:, :, 1] * (o_b @ w_br_diff[l])
                      + gt[:, :, 2] * (o_c @ w_br_gmlp[l]))
            x = x + merged @ w_out[l]
            h = rmsnorm(x, norm_xa[l])
            x = x + mem_cross_attn(h, mem_k[l], mem_v[l], xa_wq[l], xa_wo[l])
            h = rmsnorm(x, norm_ffn[l])
            x = x + jnp.square(jax.nn.relu(h @ ffn_w1[l])) @ ffn_w2[l]
            new_k.append(k.reshape(B, L, DIFF_HEADS, 2 * DIFF_DH))
            new_v.append(v)
            new_S.append(S)
            new_buf.append(buf)
            new_gv.append(gv)
        y = rmsnorm(x, norm_final)
        return (y, jnp.stack(new_k), jnp.stack(new_v), jnp.stack(new_S), jnp.stack(new_buf), jnp.stack(new_gv))

    Bp = x_prompt.shape[0]
    mk_list, mv_list = [], []
    for l in range(DEPTH):
        mn = rmsnorm(mem_prompt, norm_mem[l])
        mk_list.append((mn @ xa_wk[l]).reshape(Bp, MEM_LEN, XA_HEADS, XA_DH))
        mv_list.append((mn @ xa_wv[l]).reshape(Bp, MEM_LEN, XA_HEADS, XA_DH))
    new_mem_k_prompt = jnp.stack(mk_list)
    new_mem_v_prompt = jnp.stack(mv_list)
    S0_p = jnp.zeros((DEPTH, Bp, GDN_HEADS, GDN_DV, GDN_DK), x_prompt.dtype)
    buf0_p = jnp.zeros((DEPTH, Bp, GDN_CONV - 1, GDN_CONV_CH), x_prompt.dtype)
    pos_p = jnp.arange(x_prompt.shape[1])
    (y_prompt, new_diff_k_prompt, new_diff_v_prompt, new_state_gdn_prompt,
     new_state_gdn_conv_prompt, _gv_prompt) = trunk(x_prompt, pos_p, new_mem_k_prompt, new_mem_v_prompt,
                                                   S0_p, buf0_p, None)

    pos_s = PAST_LEN + jnp.arange(x_sample.shape[1])
    (y_sample, new_diff_k_sample, new_diff_v_sample, new_state_gdn_sample,
     new_state_gdn_conv_sample, new_gmlp_v_sample) = trunk(x_sample, pos_s, cache_mem_k, cache_mem_v,
                                                          state_gdn, state_gdn_conv,
                                                          (cache_diff_k, cache_diff_v, page_table))

    return (y_prompt, y_sample, new_diff_k_prompt, new_diff_v_prompt, new_mem_k_prompt, new_mem_v_prompt,
            new_state_gdn_prompt, new_state_gdn_conv_prompt, new_diff_k_sample, new_diff_v_sample,
            new_state_gdn_sample, new_state_gdn_conv_sample, new_gmlp_v_sample)
```

```python
import functools
import math

import jax
import jax.numpy as jnp
from jax import lax
from jax.experimental import pallas as pl
from jax.experimental.pallas import tpu as pltpu

F32 = jnp.float32
BF16 = jnp.bfloat16
EPS = 1e-6
NEG = -1e30
HIGHEST = lax.Precision.HIGHEST

D_MODEL = 1024
HEADS = 4
HW = 128
WIDTH = HEADS * HW
DIFF_DH = 64
ROPE_DIM = 16
ROPE_THETA = 500000.0
GDN_CONV = 4
GDN_CHUNK = 64
GDN_CONV_CH = 3 * WIDTH
GMLP_CHUNK = 128
PAGE = 128
SAMPLE_PAD = 8

_Z_GATES = 0
_Z_QKV = 3 * D_MODEL
_Z_GIN = _Z_QKV + GDN_CONV_CH
_Z_DQ = _Z_GIN + WIDTH
_Z_DK = _Z_DQ + WIDTH
_Z_DV = _Z_DK + WIDTH
_Z_CU = _Z_DV + WIDTH
_Z_CV = _Z_CU + WIDTH
_Z_AB = _Z_CV + WIDTH
_Z_COLS = 8192

_VMEM_LIMIT = 56 * 1024 * 1024


def _params(*sem):
    return pltpu.CompilerParams(dimension_semantics=sem, vmem_limit_bytes=_VMEM_LIMIT)


def _rms(x, g):
    return x * lax.rsqrt(jnp.mean(x * x, axis=-1, keepdims=True) + EPS) * g


def _sigmoid(x):
    return 1.0 / (1.0 + jnp.exp(-x))


def _dot(a, b):
    return jnp.dot(a, b, preferred_element_type=F32)


def _dot_nt(a, b):
    return lax.dot_general(a, b, (((1,), (1,)), ((), ())), preferred_element_type=F32)


def _dot_tn(a, b):
    return lax.dot_general(a, b, (((0,), (0,)), ((), ())), preferred_element_type=F32)


def _dot_hi(a, b):
    return jnp.dot(a, b, precision=HIGHEST, preferred_element_type=F32)


def _norm_mm_body(x_ref, g_ref, w_ref, o_ref, h_ref):
    @pl.when(pl.program_id(1) == 0)
    def _():
        h_ref[...] = _rms(x_ref[...], g_ref[...]).astype(BF16)

    o_ref[...] = _dot(h_ref[...], w_ref[...]).astype(o_ref.dtype)


def norm_matmul(x, g, w, tm, tn):
    t, d = x.shape
    n = w.shape[1]
    return pl.pallas_call(
        _norm_mm_body,
        grid=(t // tm, n // tn),
        in_specs=[pl.BlockSpec((tm, d), lambda i, j: (i, 0)),
                  pl.BlockSpec((1, d), lambda i, j: (0, 0)),
                  pl.BlockSpec((d, tn), lambda i, j: (0, j))],
        out_specs=pl.BlockSpec((tm, tn), lambda i, j: (i, j)),
        out_shape=jax.ShapeDtypeStruct((t, n), F32),
        scratch_shapes=[pltpu.VMEM((tm, d), BF16)],
        compiler_params=_params("parallel", "arbitrary"),
        name="norm_matmul",
    )(x, g.reshape(1, d), w)


def _gdn_body(qkv_ref, gin_ref, ab_ref, buf_ref, s0_ref, cw_ref, alog_ref, dtb_ref, onorm_ref,
              o_ref, sout_ref, xs_ref, ab_scr, s_ref, *, tl_in, valid):
    c = GDN_CHUNK
    t = pl.program_id(1)

    @pl.when(t == 0)
    def _():
        s_ref[...] = s0_ref[...]
        xs_ref[...] = jnp.zeros_like(xs_ref)
        xs_ref[5:8, :] = buf_ref[...]
        ab_scr[...] = jnp.zeros_like(ab_scr)

    xs_ref[8:8 + tl_in, :] = qkv_ref[...]
    ab_scr[0:tl_in, :] = ab_ref[...]
    w = cw_ref[...]
    y = (xs_ref[5:5 + c, :] * w[0:1] + xs_ref[6:6 + c, :] * w[1:2]
         + xs_ref[7:7 + c, :] * w[2:3] + xs_ref[8:8 + c, :] * w[3:4])
    xs_ref[5:8, :] = xs_ref[c + 5:c + 8, :]
    y = y * _sigmoid(y)

    ab = ab_scr[...]
    sp_in = ab + dtb_ref[...]
    softplus = jnp.maximum(sp_in, 0.0) + jnp.log1p(jnp.exp(-jnp.abs(sp_in)))
    g = -jnp.exp(alog_ref[...]) * softplus
    beta = _sigmoid(ab)
    row = lax.broadcasted_iota(jnp.int32, (c, 1), 0)
    live = row < valid
    g = jnp.where(live, g, 0.0)
    beta = jnp.where(live, beta, 0.0)

    ii = lax.broadcasted_iota(jnp.int32, (c, c), 0)
    jj = lax.broadcasted_iota(jnp.int32, (c, c), 1)
    incl = ii >= jj
    strict = ii > jj
    eye = (ii == jj).astype(F32)
    gcol = _dot_hi(incl.astype(F32), g)
    grow = _dot_hi(g.T[0:8, :], (ii <= jj).astype(F32))

    for h in range(HEADS):
        hs = slice(h * HW, (h + 1) * HW)
        gi = gcol[:, h:h + 1]
        gj = grow[h:h + 1, :]
        dec_incl = jnp.exp(jnp.where(incl, gi - gj, NEG))
        dec_strict = jnp.where(strict, dec_incl, 0.0)
        gexp = jnp.exp(gi)
        glast = gcol[c - 1:c, h:h + 1]
        bcol = beta[:, 4 + h:5 + h]

        qh = y[:, hs]
        kh = y[:, WIDTH + h * HW:WIDTH + (h + 1) * HW]
        vh = y[:, 2 * WIDTH + h * HW:2 * WIDTH + (h + 1) * HW]
        qh = qh * lax.rsqrt(jnp.sum(qh * qh, axis=-1, keepdims=True) + EPS) * (HW ** -0.5)
        kh = kh * lax.rsqrt(jnp.sum(kh * kh, axis=-1, keepdims=True) + EPS)
        qb = qh.astype(BF16)
        kb = kh.astype(BF16)

        x = -(bcol * _dot_nt(kb, kb) * dec_strict)
        tinv = eye + x
        for _ in range(int(math.log2(c)) - 1):
            x = _dot_hi(x, x)
            tinv = tinv + _dot_hi(tinv, x)
        rhs = jnp.concatenate([bcol * vh, (bcol * gexp) * kh], axis=-1)
        sol = _dot_hi(tinv, rhs)
        u0 = sol[:, :HW]
        wmat = sol[:, HW:]
        p = _dot_nt(qb, kb) * dec_incl

        s = s_ref[h]
        sb = s.astype(BF16)
        u = u0 - _dot_nt(wmat.astype(BF16), sb)
        o = gexp * _dot_nt(qb, sb) + _dot(p.astype(BF16), u.astype(BF16))
        tail = jnp.exp(glast - gi)
        s_ref[h] = jnp.exp(glast) * s + _dot_tn((u * tail).astype(BF16), kb)

        o = _rms(o, onorm_ref[...])[0:tl_in]
        gin = gin_ref[:, hs]
        o_ref[:, hs] = (o * (gin * _sigmoid(gin))).astype(o_ref.dtype)

    @pl.when(t == pl.num_programs(1) - 1)
    def _():
        sout_ref[...] = s_ref[...]


def gdn(z3, conv_buf, s0, layer, conv_w, a_log, dt_bias, o_norm, tl_in, valid, act_dtype):
    b, l, _ = z3.shape
    nt = l // tl_in
    lane_pad = lambda v, off: jnp.zeros((1, HW), F32).at[0, off:off + HEADS].set(v)
    body = functools.partial(_gdn_body, tl_in=tl_in, valid=valid)
    return pl.pallas_call(
        body,
        grid=(b, nt),
        in_specs=[pl.BlockSpec((None, tl_in, GDN_CONV_CH), lambda i, t: (i, t, _Z_QKV // GDN_CONV_CH)),
                  pl.BlockSpec((None, tl_in, WIDTH), lambda i, t: (i, t, _Z_GIN // WIDTH)),
                  pl.BlockSpec((None, tl_in, HW), lambda i, t: (i, t, _Z_AB // HW)),
                  pl.BlockSpec((None, GDN_CONV - 1, GDN_CONV_CH), lambda i, t: (i, 0, 0)),
                  pl.BlockSpec((None, None, HEADS, HW, HW), lambda i, t: (layer, i, 0, 0, 0)),
                  pl.BlockSpec((GDN_CONV, GDN_CONV_CH), lambda i, t: (0, 0)),
                  pl.BlockSpec((1, HW), lambda i, t: (0, 0)),
                  pl.BlockSpec((1, HW), lambda i, t: (0, 0)),
                  pl.BlockSpec((1, HW), lambda i, t: (0, 0))],
        out_specs=[pl.BlockSpec((None, tl_in, WIDTH), lambda i, t: (i, t, 0)),
                   pl.BlockSpec((None, HEADS, HW, HW), lambda i, t: (i, 0, 0, 0))],
        out_shape=[jax.ShapeDtypeStruct((b, l, WIDTH), act_dtype),
                   jax.ShapeDtypeStruct((b, HEADS, HW, HW), F32)],
        scratch_shapes=[pltpu.VMEM((GDN_CHUNK + 8, GDN_CONV_CH), F32),
                        pltpu.VMEM((GDN_CHUNK, HW), F32),
                        pltpu.VMEM((HEADS, HW, HW), F32)],
        compiler_params=_params("parallel", "arbitrary"),
        name="gdn",
    )(z3, z3, z3, conv_buf, s0, conv_w, lane_pad(a_log, 0), lane_pad(dt_bias, 0), o_norm.reshape(1, HW))


def _rope_tables(pos):
    half = ROPE_DIM // 2
    inv = ROPE_THETA ** (-jnp.arange(0, ROPE_DIM, 2, dtype=F32) / ROPE_DIM)
    ang = pos.astype(F32)[:, None] * inv[None, :]
    m = jnp.arange(HW) % DIFF_DH
    cos = jnp.cos(ang)[:, m % half]
    sin = jnp.sin(ang)[:, m % half]
    cos_t = jnp.where(m < ROPE_DIM, cos, 1.0)
    sin_a = jnp.where(m < half, -sin, 0.0)
    sin_b = jnp.where((m >= half) & (m < ROPE_DIM), sin, 0.0)
    return cos_t.astype(F32), sin_a.astype(F32), sin_b.astype(F32)


def _rope_body(q_ref, k_ref, v_ref, cos_ref, sa_ref, sb_ref, qm_ref, kr_ref, kb_ref, vf_ref, vb_ref):
    half = ROPE_DIM // 2
    cos, sa, sb = cos_ref[...], sa_ref[...], sb_ref[...]
    lane = lax.broadcasted_iota(jnp.int32, (1, HW), 1)
    first_map = lane < DIFF_DH

    def rot(x):
        return x * cos + pltpu.roll(x, HW - half, 1) * sa + pltpu.roll(x, half, 1) * sb

    for h in range(HEADS):
        hs = slice(h * HW, (h + 1) * HW)
        q = rot(q_ref[:, hs]) * (DIFF_DH ** -0.5)
        qm_ref[0, :, hs] = jnp.where(first_map, q, 0.0).astype(qm_ref.dtype)
        qm_ref[1, :, hs] = jnp.where(first_map, 0.0, q).astype(qm_ref.dtype)
        k = rot(k_ref[:, hs])
        kr_ref[:, hs] = k
        kb_ref[:, hs] = k.astype(kb_ref.dtype)
    v = v_ref[...]
    vf_ref[...] = v
    vb_ref[...] = v.astype(vb_ref.dtype)


def rope_split(z, tables, tr, act_dtype):
    t = z.shape[0]
    nl = tables[0].shape[0] // tr
    zspec = lambda off: pl.BlockSpec((tr, WIDTH), lambda i: (i, off // WIDTH))
    tspec = pl.BlockSpec((tr, HW), lambda i: (i % nl, 0))
    ospec = pl.BlockSpec((tr, WIDTH), lambda i: (i, 0))
    return pl.pallas_call(
        _rope_body,
        grid=(t // tr,),
        in_specs=[zspec(_Z_DQ), zspec(_Z_DK), zspec(_Z_DV), tspec, tspec, tspec],
        out_specs=[pl.BlockSpec((2, tr, WIDTH), lambda i: (0, i, 0)), ospec, ospec, ospec, ospec],
        out_shape=[jax.ShapeDtypeStruct((2, t, WIDTH), act_dtype),
                   jax.ShapeDtypeStruct((t, WIDTH), F32),
                   jax.ShapeDtypeStruct((t, WIDTH), act_dtype),
                   jax.ShapeDtypeStruct((t, WIDTH), F32),
                   jax.ShapeDtypeStruct((t, WIDTH), act_dtype)],
        compiler_params=_params("parallel"),
        name="rope_split",
    )(z, z, z, *tables)


def _lambda(lq1_ref, lk1_ref, lq2_ref, lk2_ref, lam_init):
    s1 = jnp.sum(lq1_ref[...] * lk1_ref[...], axis=-1, keepdims=True)
    s2 = jnp.sum(lq2_ref[...] * lk2_ref[...], axis=-1, keepdims=True)
    return jnp.exp(s1) - jnp.exp(s2) + lam_init


def _online_update(s, v, m_prev, l_prev, acc_prev):
    m_new = jnp.maximum(m_prev, jnp.max(s, axis=-1, keepdims=True))
    alpha = jnp.exp(m_prev - m_new)
    p = jnp.exp(s - m_new)
    l_new = alpha * l_prev + jnp.sum(p, axis=-1, keepdims=True)
    acc_new = alpha * acc_prev + _dot(p.astype(BF16), v)
    return m_new, l_new, acc_new


def _flash_body(q_ref, k_ref, v_ref, lq1_ref, lk1_ref, lq2_ref, lk2_ref, sub_ref, o_ref,
                m_ref, l_ref, acc_ref, *, lam_init):
    qi = pl.program_id(2)
    ki = pl.program_id(3)
    tq = q_ref.shape[1]

    @pl.when(ki == 0)
    def _():
        m_ref[...] = jnp.full_like(m_ref, NEG)
        l_ref[...] = jnp.zeros_like(l_ref)
        acc_ref[...] = jnp.zeros_like(acc_ref)

    def block(masked):
        k = k_ref[...]
        v = v_ref[...]
        for c in range(2):
            s = _dot_nt(q_ref[c], k)
            if masked:
                ri = lax.broadcasted_iota(jnp.int32, s.shape, 0)
                ci = lax.broadcasted_iota(jnp.int32, s.shape, 1)
                s = jnp.where(ci <= ri, s, NEG)
            m_ref[c], l_ref[c], acc_ref[c] = _online_update(s, v, m_ref[c], l_ref[c], acc_ref[c])

    pl.when(ki < qi)(functools.partial(block, False))
    pl.when(ki == qi)(functools.partial(block, True))

    @pl.when(ki == pl.num_programs(3) - 1)
    def _():
        lam = _lambda(lq1_ref, lk1_ref, lq2_ref, lk2_ref, lam_init)
        o = acc_ref[0] * (1.0 / l_ref[0]) - lam * (acc_ref[1] * (1.0 / l_ref[1]))
        o_ref[...] = (_rms(o, sub_ref[...]) * (1.0 - lam_init)).astype(o_ref.dtype)


def flash_diff(qm, kb, vb, lparams, subln, lam_init, batch, tq):
    t = kb.shape[0]
    l = t // batch
    nq = l // tq
    lspec = pl.BlockSpec((1, DIFF_DH), lambda b, h, qi, ki: (0, 0))
    kvspec = pl.BlockSpec((tq, HW), lambda b, h, qi, ki: (b * nq + jnp.minimum(ki, qi), h))
    return pl.pallas_call(
        functools.partial(_flash_body, lam_init=lam_init),
        grid=(batch, HEADS, nq, nq),
        in_specs=[pl.BlockSpec((2, tq, HW), lambda b, h, qi, ki: (0, b * nq + qi, h)),
                  kvspec, kvspec, lspec, lspec, lspec, lspec,
                  pl.BlockSpec((1, HW), lambda b, h, qi, ki: (0, 0))],
        out_specs=pl.BlockSpec((tq, HW), lambda b, h, qi, ki: (b * nq + qi, h)),
        out_shape=jax.ShapeDtypeStruct((t, WIDTH), BF16),
        scratch_shapes=[pltpu.VMEM((2, tq, 1), F32), pltpu.VMEM((2, tq, 1), F32),
                        pltpu.VMEM((2, tq, HW), F32)],
        compiler_params=_params("parallel", "parallel", "parallel", "arbitrary"),
        name="flash_diff",
    )(qm, kb, vb, *[p.reshape(1, DIFF_DH) for p in lparams], subln.reshape(1, HW))


_PAGES_PER_STEP = 4


def _paged_body(pt_ref, q_ref, *refs, lam_init, n_valid):
    del pt_ref
    pp = _PAGES_PER_STEP
    k_refs, v_refs = refs[:pp], refs[pp:2 * pp]
    (kn_ref, vn_ref, lq1_ref, lk1_ref, lq2_ref, lk2_ref, sub_ref, o_ref,
     m_ref, l_ref, acc_ref, kn_scr, vn_scr) = refs[2 * pp:]
    p_idx = pl.program_id(1)
    nq = q_ref.shape[1]

    @pl.when(p_idx == 0)
    def _():
        m_ref[...] = jnp.full_like(m_ref, NEG)
        l_ref[...] = jnp.zeros_like(l_ref)
        acc_ref[...] = jnp.zeros_like(acc_ref)

    def head_q(h):
        hs = slice(h * HW, (h + 1) * HW)
        return jnp.concatenate([q_ref[0, :, hs], q_ref[1, :, hs]], axis=0).astype(BF16)

    def update(k_pages, v_pages, mask):
        for h in range(HEADS):
            hs = slice(h * HW, (h + 1) * HW)
            rs = slice(h * 2 * nq, (h + 1) * 2 * nq)
            q2 = head_q(h)
            s = jnp.concatenate([_dot_nt(q2, kp[:, hs].astype(BF16)) for kp in k_pages], axis=-1)
            if mask is not None:
                s = jnp.where(mask, s, NEG)
            m_prev = m_ref[rs]
            m_new = jnp.maximum(m_prev, jnp.max(s, axis=-1, keepdims=True))
            alpha = jnp.exp(m_prev - m_new)
            p = jnp.exp(s - m_new).astype(BF16)
            l_ref[rs] = alpha * l_ref[rs] + jnp.sum(p.astype(F32), axis=-1, keepdims=True)
            pv = _dot(p[:, 0:PAGE], v_pages[0][:, hs].astype(BF16))
            for j in range(1, len(v_pages)):
                pv = pv + _dot(p[:, j * PAGE:(j + 1) * PAGE], v_pages[j][:, hs].astype(BF16))
            acc_ref[rs] = alpha * acc_ref[rs] + pv
            m_ref[rs] = m_new

    update([r[...] for r in k_refs], [r[...] for r in v_refs], None)

    @pl.when(p_idx == pl.num_programs(1) - 1)
    def _():
        kn_scr[...] = jnp.zeros_like(kn_scr)
        vn_scr[...] = jnp.zeros_like(vn_scr)
        kn_scr[0:nq, :] = kn_ref[...]
        vn_scr[0:nq, :] = vn_ref[...]
        ri = lax.broadcasted_iota(jnp.int32, (2 * nq, PAGE), 0) % nq
        ci = lax.broadcasted_iota(jnp.int32, (2 * nq, PAGE), 1)
        update([kn_scr[...]], [vn_scr[...]], (ci <= ri) & (ci < n_valid))
        lam = _lambda(lq1_ref, lk1_ref, lq2_ref, lk2_ref, lam_init)
        for h in range(HEADS):
            hs = slice(h * HW, (h + 1) * HW)
            r0 = slice(h * 2 * nq, h * 2 * nq + nq)
            r1 = slice(h * 2 * nq + nq, (h + 1) * 2 * nq)
            o = acc_ref[r0] * (1.0 / l_ref[r0]) - lam * (acc_ref[r1] * (1.0 / l_ref[r1]))
            o_ref[:, hs] = (_rms(o, sub_ref[...]) * (1.0 - lam_init)).astype(o_ref.dtype)


def paged_diff(qm, kb, vb, cache_k, cache_v, page_table, layer, lparams, subln, lam_init, n_valid):
    _, b, nq, _ = qm.shape
    n_pages = page_table.shape[1]
    pp = _PAGES_PER_STEP
    assert n_pages % pp == 0
    page_spec = lambda j: pl.BlockSpec((None, None, PAGE, WIDTH),
                                       lambda i, p, pt: (layer, pt[i, p * pp + j], 0, 0))
    new_spec = pl.BlockSpec((None, nq, WIDTH), lambda i, p, pt: (i, 0, 0))
    lspec = pl.BlockSpec((1, DIFF_DH), lambda i, p, pt: (0, 0))
    grid_spec = pltpu.PrefetchScalarGridSpec(
        num_scalar_prefetch=1,
        grid=(b, n_pages // pp),
        in_specs=([pl.BlockSpec((2, None, nq, WIDTH), lambda i, p, pt: (0, i, 0, 0))]
                  + [page_spec(j) for j in range(pp)] + [page_spec(j) for j in range(pp)]
                  + [new_spec, new_spec, lspec, lspec, lspec, lspec,
                     pl.BlockSpec((1, HW), lambda i, p, pt: (0, 0))]),
        out_specs=pl.BlockSpec((None, nq, WIDTH), lambda i, p, pt: (i, 0, 0)),
        scratch_shapes=[pltpu.VMEM((HEADS * 2 * nq, 1), F32), pltpu.VMEM((HEADS * 2 * nq, 1), F32),
                        pltpu.VMEM((HEADS * 2 * nq, HW), F32),
                        pltpu.VMEM((PAGE, WIDTH), F32), pltpu.VMEM((PAGE, WIDTH), F32)],
    )
    return pl.pallas_call(
        functools.partial(_paged_body, lam_init=lam_init, n_valid=n_valid),
        grid_spec=grid_spec,
        out_shape=jax.ShapeDtypeStruct((b, nq, WIDTH), F32),
        compiler_params=_params("parallel", "arbitrary"),
        name="paged_diff",
    )(page_table, qm, *([cache_k] * pp), *([cache_v] * pp), kb, vb,
      *[p.reshape(1, DIFF_DH) for p in lparams], subln.reshape(1, HW))


def _gelu(x):
    return 0.5 * x * (1.0 + jnp.tanh(math.sqrt(2.0 / math.pi) * (x + 0.044715 * (x * x * x))))


def _gmlp_body(u_ref, v_ref, lg_ref, lb_ref, ws_ref, bst_ref, o_ref, gv_ref, v_scr, *, tr):
    u = _gelu(u_ref[...])
    v = _gelu(v_ref[...])
    mu = jnp.mean(v, axis=-1, keepdims=True)
    vc = v - mu
    v = vc * lax.rsqrt(jnp.mean(vc * vc, axis=-1, keepdims=True) + EPS) * lg_ref[...] + lb_ref[...]
    gv_ref[...] = v
    if tr < GMLP_CHUNK:
        v_scr[...] = jnp.zeros_like(v_scr)
        v_scr[0:tr, :] = v
    ii = lax.broadcasted_iota(jnp.int32, (tr, GMLP_CHUNK), 0)
    jj = lax.broadcasted_iota(jnp.int32, (tr, GMLP_CHUNK), 1)
    for g in range(HEADS):
        gs = slice(g * HW, (g + 1) * HW)
        w = jnp.where(ii >= jj, ws_ref[g, 0:tr, :], 0.0).astype(BF16)
        vg = (v_scr[:, gs] if tr < GMLP_CHUNK else v[:, gs]).astype(BF16)
        mixed = _dot(w, vg) + bst_ref[0:tr, g:g + 1]
        o_ref[:, gs] = (u[:, gs] * mixed).astype(o_ref.dtype)


def gmlp(z, ln_g, ln_b, ws, bs, tr, act_dtype):
    t = z.shape[0]
    zspec = lambda off: pl.BlockSpec((tr, WIDTH), lambda i: (i, off // WIDTH))
    ospec = pl.BlockSpec((tr, WIDTH), lambda i: (i, 0))
    vec = pl.BlockSpec((1, WIDTH), lambda i: (0, 0))
    return pl.pallas_call(
        functools.partial(_gmlp_body, tr=tr),
        grid=(t // tr,),
        in_specs=[zspec(_Z_CU), zspec(_Z_CV), vec, vec,
                  pl.BlockSpec((HEADS, GMLP_CHUNK, GMLP_CHUNK), lambda i: (0, 0, 0)),
                  pl.BlockSpec((GMLP_CHUNK, HEADS), lambda i: (0, 0))],
        out_specs=[ospec, ospec],
        out_shape=[jax.ShapeDtypeStruct((t, WIDTH), act_dtype), jax.ShapeDtypeStruct((t, WIDTH), F32)],
        scratch_shapes=[pltpu.VMEM((GMLP_CHUNK, WIDTH), F32)],
        compiler_params=_params("parallel"),
        name="gmlp",
    )(z, z, ln_g.reshape(1, WIDTH), ln_b.reshape(1, WIDTH), ws, bs.T)


def _merge_body(x_ref, gate_ref, oa_ref, ob_ref, oc_ref, wa_ref, wb_ref, wc_ref, wo_ref, o_ref):
    d = D_MODEL
    merged = (_sigmoid(gate_ref[:, 0:d]) * _dot(oa_ref[...].astype(BF16), wa_ref[...])
              + _sigmoid(gate_ref[:, d:2 * d]) * _dot(ob_ref[...].astype(BF16), wb_ref[...])
              + _sigmoid(gate_ref[:, 2 * d:3 * d]) * _dot(oc_ref[...].astype(BF16), wc_ref[...]))
    o_ref[...] = x_ref[...] + _dot(merged.astype(BF16), wo_ref[...])


def merge(x, z, oa, ob, oc, wa, wb, wc, wo, tm):
    t, d = x.shape
    row = lambda w: pl.BlockSpec((tm, w), lambda i: (i, 0))
    full = lambda a: pl.BlockSpec(a.shape, lambda i: (0, 0))
    return pl.pallas_call(
        _merge_body,
        grid=(t // tm,),
        in_specs=[row(d), row(3 * d), row(WIDTH), row(WIDTH), row(WIDTH),
                  full(wa), full(wb), full(wc), full(wo)],
        out_specs=row(d),
        out_shape=jax.ShapeDtypeStruct((t, d), F32),
        compiler_params=_params("parallel"),
        name="merge",
    )(x, z, oa, ob, oc, wa, wb, wc, wo)


def _xattn_body(x_ref, g_ref, wq_ref, mk_ref, mv_ref, wo_ref, o_ref):
    x = x_ref[...]
    q = _dot(_rms(x, g_ref[...]).astype(BF16), wq_ref[...])
    outs = []
    for h in range(HEADS):
        hs = slice(h * HW, (h + 1) * HW)
        s = _dot_nt(q[:, hs].astype(BF16), mk_ref[:, hs].astype(BF16)) * (HW ** -0.5)
        e = jnp.exp(s - jnp.max(s, axis=-1, keepdims=True))
        p = e * (1.0 / jnp.sum(e, axis=-1, keepdims=True))
        outs.append(_dot(p.astype(BF16), mv_ref[:, hs].astype(BF16)))
    o = jnp.concatenate(outs, axis=-1)
    o_ref[...] = x + _dot(o.astype(BF16), wo_ref[...])


def xattn(x, g, wq, mem_k, mem_v, layer, wo, tm, tiles_per_seq):
    t, d = x.shape
    m = mem_k.shape[2]
    mem_spec = pl.BlockSpec((None, None, m, WIDTH), lambda i: (layer, i // tiles_per_seq, 0, 0))
    return pl.pallas_call(
        _xattn_body,
        grid=(t // tm,),
        in_specs=[pl.BlockSpec((tm, d), lambda i: (i, 0)),
                  pl.BlockSpec((1, d), lambda i: (0, 0)),
                  pl.BlockSpec(wq.shape, lambda i: (0, 0)),
                  mem_spec, mem_spec,
                  pl.BlockSpec(wo.shape, lambda i: (0, 0))],
        out_specs=pl.BlockSpec((tm, d), lambda i: (i, 0)),
        out_shape=jax.ShapeDtypeStruct((t, d), F32),
        compiler_params=_params("parallel"),
        name="xattn",
    )(x, g.reshape(1, d), wq, mem_k, mem_v, wo)


def _ffn_body(x_ref, g_ref, w1_ref, w2_ref, o_ref, h_ref):
    @pl.when(pl.program_id(1) == 0)
    def _():
        x = x_ref[...]
        h_ref[...] = _rms(x, g_ref[...]).astype(BF16)
        o_ref[...] = x

    a = jnp.maximum(_dot(h_ref[...], w1_ref[...]), 0.0)
    o_ref[...] += _dot((a * a).astype(BF16), w2_ref[...])


def ffn(x, g, w1, w2, tm, tf):
    t, d = x.shape
    f = w1.shape[1]
    return pl.pallas_call(
        _ffn_body,
        grid=(t // tm, f // tf),
        in_specs=[pl.BlockSpec((tm, d), lambda i, j: (i, 0)),
                  pl.BlockSpec((1, d), lambda i, j: (0, 0)),
                  pl.BlockSpec((d, tf), lambda i, j: (0, j)),
                  pl.BlockSpec((tf, d), lambda i, j: (j, 0))],
        out_specs=pl.BlockSpec((tm, d), lambda i, j: (i, 0)),
        out_shape=jax.ShapeDtypeStruct((t, d), F32),
        scratch_shapes=[pltpu.VMEM((tm, d), BF16)],
        compiler_params=_params("parallel", "arbitrary"),
        name="ffn",
    )(x, g.reshape(1, d), w1, w2)


def _final_norm_body(x_ref, g_ref, o_ref):
    o_ref[...] = _rms(x_ref[...], g_ref[...])


def final_norm(x, g, tm):
    t, d = x.shape
    return pl.pallas_call(
        _final_norm_body,
        grid=(t // tm,),
        in_specs=[pl.BlockSpec((tm, d), lambda i: (i, 0)), pl.BlockSpec((1, d), lambda i: (0, 0))],
        out_specs=pl.BlockSpec((tm, d), lambda i: (i, 0)),
        out_shape=jax.ShapeDtypeStruct((t, d), F32),
        compiler_params=_params("parallel"),
        name="final_norm",
    )(x, g.reshape(1, d))


def _regroup_w_in(w):
    o = 0
    parts = {}
    for name, size in (("q", WIDTH), ("k", WIDTH), ("v", WIDTH), ("a", HEADS), ("b", HEADS), ("g", WIDTH),
                       ("dq", WIDTH), ("dk", WIDTH), ("dv", WIDTH), ("cu", WIDTH), ("cv", WIDTH),
                       ("gates", 3 * D_MODEL)):
        parts[name] = w[:, o:o + size]
        o += size
    used = _Z_AB + 2 * HEADS
    cols = [parts[n] for n in ("gates", "q", "k", "v", "g", "dq", "dk", "dv", "cu", "cv", "a", "b")]
    cols.append(jnp.zeros((w.shape[0], _Z_COLS - used), w.dtype))
    return jnp.concatenate(cols, axis=1).astype(BF16)


def _tile(n, want):
    while n % want:
        want //= 2
    return want


def kernel(x_prompt, x_sample, cache_diff_k, cache_diff_v, page_table, cache_mem_k, cache_mem_v, state_gdn, state_gdn_conv, mem_prompt, w_in, norm_mix, gdn_conv_w, gdn_A_log, gdn_dt_bias, gdn_o_norm, diff_lq1, diff_lk1, diff_lq2, diff_lk2, diff_subln, gmlp_ln_g, gmlp_ln_b, gmlp_ws, gmlp_b, w_br_gdn, w_br_diff, w_br_gmlp, w_out, norm_xa, norm_mem, xa_wq, xa_wk, xa_wv, xa_wo, norm_ffn, ffn_w1, ffn_w2, norm_final):
    depth = w_in.shape[0]
    bp, lp, d = x_prompt.shape
    bs, ls, _ = x_sample.shape
    mem_len = mem_prompt.shape[1]
    past_len = page_table.shape[1] * PAGE
    lpad = SAMPLE_PAD
    bf = lambda a: a.astype(BF16)

    mem_tok = mem_prompt.reshape(bp * mem_len, d)
    mem_kv = [norm_matmul(mem_tok, norm_mem[l], bf(jnp.concatenate([xa_wk[l], xa_wv[l]], axis=1)),
                          _tile(bp * mem_len, 512), 512) for l in range(depth)]
    mem_k_p = jnp.stack([kv[:, :WIDTH] for kv in mem_kv]).reshape(depth, bp, mem_len, WIDTH)
    mem_v_p = jnp.stack([kv[:, WIDTH:] for kv in mem_kv]).reshape(depth, bp, mem_len, WIDTH)

    weights = [dict(w_in=_regroup_w_in(w_in[l]), wa=bf(w_br_gdn[l]), wb=bf(w_br_diff[l]), wc=bf(w_br_gmlp[l]),
                    wo=bf(w_out[l]), xq=bf(xa_wq[l]), xo=bf(xa_wo[l]), w1=bf(ffn_w1[l]), w2=bf(ffn_w2[l]))
               for l in range(depth)]

    def trunk(x, b, l, l_valid, pos, mem_k, mem_v, s_in, buf_in, paged):
        t = b * l
        prompt = paged is None
        act = BF16 if prompt else F32
        tm = _tile(t, 1024)
        tables = _rope_tables(pos)
        outs = dict(k=[], v=[], s=[], buf=[], gv=[])
        for li in range(depth):
            w = weights[li]
            lam_init = 0.8 - 0.6 * math.exp(-0.3 * li)
            lparams = (diff_lq1[li], diff_lk1[li], diff_lq2[li], diff_lk2[li])
            z = norm_matmul(x, norm_mix[li], w["w_in"], tm, 1024)
            z3 = z.reshape(b, l, _Z_COLS)
            o_a, s_new = gdn(z3, buf_in[li], s_in, li, gdn_conv_w[li], gdn_A_log[li], gdn_dt_bias[li],
                             gdn_o_norm[li], GDN_CHUNK if prompt else l, l_valid, act)
            outs["buf"].append(z3[:, l_valid - (GDN_CONV - 1):l_valid, _Z_QKV:_Z_QKV + GDN_CONV_CH])
            qm, kr, kb, vf, vb = rope_split(z, tables, _tile(l, 512), act)
            if prompt:
                o_b = flash_diff(qm, kb, vb, lparams, diff_subln[li], lam_init, b, _tile(l, 512))
            else:
                ck, cv, pt = paged
                o_b = paged_diff(qm.reshape(2, b, l, WIDTH), kb.reshape(b, l, WIDTH), vb.reshape(b, l, WIDTH),
                                 ck, cv, pt, li, lparams, diff_subln[li], lam_init, l_valid)
                o_b = o_b.reshape(t, WIDTH)
            o_c, gv = gmlp(z, gmlp_ln_g[li], gmlp_ln_b[li], gmlp_ws[li], gmlp_b[li], min(l, GMLP_CHUNK), act)
            x = merge(x, z, o_a.reshape(t, WIDTH), o_b, o_c, w["wa"], w["wb"], w["wc"], w["wo"], _tile(t, 512))
            xa_tm = _tile(l, 512)
            x = xattn(x, norm_xa[li], w["xq"], mem_k, mem_v, li, w["xo"], xa_tm, l // xa_tm)
            x = ffn(x, norm_ffn[li], w["w1"], w["w2"], tm, 1024)
            outs["k"].append(kr)
            outs["v"].append(vf)
            outs["s"].append(s_new)
            outs["gv"].append(gv)
        y = final_norm(x, norm_final, tm)
        seq = lambda a: a.reshape(depth, b, l, -1)[:, :, :l_valid]
        return (y.reshape(b, l, d)[:, :l_valid],
                seq(jnp.stack(outs["k"])).reshape(depth, b, l_valid, HEADS, HW),
                seq(jnp.stack(outs["v"])).reshape(depth, b, l_valid, HEADS, HW),
                jnp.stack(outs["s"]), jnp.stack(outs["buf"]), seq(jnp.stack(outs["gv"])))

    s0_p = jnp.zeros((depth, bp, HEADS, HW, HW), F32)
    buf0_p = jnp.zeros((depth, bp, GDN_CONV - 1, GDN_CONV_CH), F32)
    (y_p, k_p, v_p, s_p, buf_p, _) = trunk(x_prompt.reshape(bp * lp, d), bp, lp, lp, jnp.arange(lp),
                                           mem_k_p, mem_v_p, s0_p, buf0_p, None)

    x_s = jnp.pad(x_sample, ((0, 0), (0, lpad - ls), (0, 0))).reshape(bs * lpad, d)
    n_pool = cache_diff_k.shape[1]
    paged = (cache_diff_k.reshape(depth, n_pool, PAGE, WIDTH), cache_diff_v.reshape(depth, n_pool, PAGE, WIDTH),
             page_table)
    (y_s, k_s, v_s, s_s, buf_s, gv_s) = trunk(x_s, bs, lpad, ls, past_len + jnp.arange(lpad),
                                              cache_mem_k.reshape(depth, bs, mem_len, WIDTH),
                                              cache_mem_v.reshape(depth, bs, mem_len, WIDTH),
                                              state_gdn, state_gdn_conv, paged)

    return (y_p, y_s, k_p, v_p,
            mem_k_p.reshape(depth, bp, mem_len, HEADS, HW), mem_v_p.reshape(depth, bp, mem_len, HEADS, HW),
            s_p, buf_p, k_s, v_s, s_s, buf_s, gv_s)
```

```python
import functools
import math

import jax
import jax.numpy as jnp
from jax import lax
from jax.experimental import pallas as pl
from jax.experimental.pallas import tpu as pltpu

F32 = jnp.float32
BF16 = jnp.bfloat16
EPS = 1e-6
NEG = -1e30

D_MODEL = 1024
HEADS = 4
HW = 128
WIDTH = HEADS * HW
DIFF_DH = 64
ROPE_DIM = 16
ROPE_THETA = 500000.0
GDN_CONV = 4
GDN_CHUNK = 64
GDN_CONV_CH = 3 * WIDTH
GMLP_CHUNK = 128
PAGE = 128
SAMPLE_PAD = 8

_Z_GATES = 0
_Z_QKV = 3 * D_MODEL
_Z_GIN = _Z_QKV + GDN_CONV_CH
_Z_DQ = _Z_GIN + WIDTH
_Z_DK = _Z_DQ + WIDTH
_Z_DV = _Z_DK + WIDTH
_Z_CU = _Z_DV + WIDTH
_Z_CV = _Z_CU + WIDTH
_Z_AB = _Z_CV + WIDTH
_Z_COLS = 8192

_VMEM_LIMIT = 56 * 1024 * 1024


def _params(*sem):
    return pltpu.CompilerParams(dimension_semantics=sem, vmem_limit_bytes=_VMEM_LIMIT)


def _rms(x, g):
    return x * lax.rsqrt(jnp.mean(x * x, axis=-1, keepdims=True) + EPS) * g


def _sigmoid(x):
    return 1.0 / (1.0 + jnp.exp(-x))


def _dot(a, b):
    return jnp.dot(a, b, preferred_element_type=F32)


def _dot_nt(a, b):
    return lax.dot_general(a, b, (((1,), (1,)), ((), ())), preferred_element_type=F32)


def _dot_tn(a, b):
    return lax.dot_general(a, b, (((0,), (0,)), ((), ())), preferred_element_type=F32)


def _split(a):
    hi = a.astype(BF16).astype(F32)
    return hi, a - hi


def _hi_lhs(a):
    hi, lo = _split(a)
    return jnp.concatenate([hi, hi, lo], axis=1).astype(BF16)


def _hi_rhs(b):
    hi, lo = _split(b)
    return jnp.concatenate([hi, lo, hi], axis=0).astype(BF16)


def _dot_hi(a, b):
    return _dot(_hi_lhs(a), _hi_rhs(b))


def _dot_hi_exact_lhs(a_bf16, b):
    hi, lo = _split(b)
    lo_hi, lo_lo = _split(lo)
    return _dot(jnp.concatenate([a_bf16] * 3, axis=1), jnp.concatenate([hi, lo_hi, lo_lo], axis=0).astype(BF16))


def _dot_hi_exact_rhs(a, b_bf16):
    hi, lo = _split(a)
    lo_hi, lo_lo = _split(lo)
    return _dot(jnp.concatenate([hi, lo_hi, lo_lo], axis=1).astype(BF16), jnp.concatenate([b_bf16] * 3, axis=0))


def _norm_mm_body(x_ref, g_ref, w_ref, o_ref, h_ref):
    @pl.when(pl.program_id(1) == 0)
    def _():
        h_ref[...] = _rms(x_ref[...], g_ref[...]).astype(BF16)

    o_ref[...] = _dot(h_ref[...], w_ref[...]).astype(o_ref.dtype)


def norm_matmul(x, g, w, tm, tn):
    t, d = x.shape
    n = w.shape[1]
    return pl.pallas_call(
        _norm_mm_body,
        grid=(t // tm, n // tn),
        in_specs=[pl.BlockSpec((tm, d), lambda i, j: (i, 0)),
                  pl.BlockSpec((1, d), lambda i, j: (0, 0)),
                  pl.BlockSpec((d, tn), lambda i, j: (0, j))],
        out_specs=pl.BlockSpec((tm, tn), lambda i, j: (i, j)),
        out_shape=jax.ShapeDtypeStruct((t, n), F32),
        scratch_shapes=[pltpu.VMEM((tm, d), BF16)],
        compiler_params=_params("parallel", "arbitrary"),
        name="norm_matmul",
    )(x, g.reshape(1, d), w)


def _gdn_body(qkv_ref, gin_ref, ab_ref, buf_ref, s0_ref, cw_ref, alog_ref, dtb_ref, onorm_ref,
              o_ref, sout_ref, xs_ref, ab_scr, s_ref, *, tl, tl_in, valid):
    c = GDN_CHUNK
    nc = tl // c
    t = pl.program_id(1)

    @pl.when(t == 0)
    def _():
        s_ref[...] = s0_ref[...]
        xs_ref[...] = jnp.zeros_like(xs_ref)
        xs_ref[5:8, :] = buf_ref[...]
        ab_scr[...] = jnp.zeros_like(ab_scr)

    xs_ref[8:8 + tl_in, :] = qkv_ref[...]
    ab_scr[0:tl_in, :] = ab_ref[...]
    w = cw_ref[...]
    y = (xs_ref[5:5 + tl, :] * w[0:1] + xs_ref[6:6 + tl, :] * w[1:2]
         + xs_ref[7:7 + tl, :] * w[2:3] + xs_ref[8:8 + tl, :] * w[3:4])
    xs_ref[5:8, :] = xs_ref[tl + 5:tl + 8, :]
    y = y * _sigmoid(y)

    ab = ab_scr[...]
    sp_in = ab + dtb_ref[...]
    softplus = jnp.maximum(sp_in, 0.0) + jnp.log1p(jnp.exp(-jnp.abs(sp_in)))
    g = -jnp.exp(alog_ref[...]) * softplus
    beta = _sigmoid(ab)
    if valid < tl:
        live = lax.broadcasted_iota(jnp.int32, (tl, 1), 0) < valid
        g = jnp.where(live, g, 0.0)
        beta = jnp.where(live, beta, 0.0)

    ti = lax.broadcasted_iota(jnp.int32, (tl, tl), 0)
    tj = lax.broadcasted_iota(jnp.int32, (tl, tl), 1)
    same_chunk = (ti // c) == (tj // c)
    gcol = _dot_hi_exact_lhs(jnp.where(same_chunk & (ti >= tj), 1.0, 0.0).astype(BF16), g)
    grow = _dot_hi_exact_rhs(g.T[0:8, :], jnp.where(same_chunk & (ti <= tj), 1.0, 0.0).astype(BF16))

    ii = lax.broadcasted_iota(jnp.int32, (c, c), 0)
    jj = lax.broadcasted_iota(jnp.int32, (c, c), 1)
    incl = ii >= jj
    strict = ii > jj
    eye = (ii == jj).astype(F32)

    def prepare(ci, h):
        rs = slice(ci * c, (ci + 1) * c)
        gi = gcol[rs, h:h + 1]
        gj = grow[h:h + 1, rs]
        dec_incl = jnp.exp(jnp.where(incl, gi - gj, NEG))
        glast = gcol[(ci + 1) * c - 1:(ci + 1) * c, h:h + 1]
        qh = y[rs, h * HW:(h + 1) * HW]
        kh = y[rs, WIDTH + h * HW:WIDTH + (h + 1) * HW]
        qh = qh * lax.rsqrt(jnp.sum(qh * qh, axis=-1, keepdims=True) + EPS) * (HW ** -0.5)
        kh = kh * lax.rsqrt(jnp.sum(kh * kh, axis=-1, keepdims=True) + EPS)
        return dict(dec_incl=dec_incl, dec_strict=jnp.where(strict, dec_incl, 0.0), gexp=jnp.exp(gi),
                    bcol=beta[rs, 4 + h:5 + h], kh=kh, vh=y[rs, 2 * WIDTH + h * HW:2 * WIDTH + (h + 1) * HW],
                    qb=qh.astype(BF16), kb=kh.astype(BF16), tail=jnp.exp(glast - gi), gend=jnp.exp(glast))

    terms = [prepare(ci, h) for ci in range(nc) for h in range(HEADS)]
    for m in terms:
        kk = _dot_nt(m["kb"], m["kb"])
        m["p"] = (_dot_nt(m["qb"], m["kb"]) * m["dec_incl"]).astype(BF16)
        m["x"] = -(m["bcol"] * kk * m["dec_strict"])
        m["tinv"] = eye + m["x"]
    for m in terms:
        m["x"] = _dot_hi(m["x"], m["x"])
    for _ in range(int(math.log2(c)) - 2):
        for m in terms:
            r = _dot_hi(jnp.concatenate([m["x"], m["tinv"]], axis=0), m["x"])
            m["x"] = r[0:c]
            m["tinv"] = m["tinv"] + r[c:2 * c]
    for m in terms:
        m["tinv"] = m["tinv"] + _dot_hi(m["tinv"], m["x"])
    for m in terms:
        sol = _dot_hi(m["tinv"], jnp.concatenate([m["bcol"] * m["vh"], (m["bcol"] * m["gexp"]) * m["kh"]], axis=-1))
        m["u0"] = sol[:, :HW]
        m["w"] = sol[:, HW:].astype(BF16)

    state = [s_ref[h] for h in range(HEADS)]
    for ci in range(nc):
        ms = terms[ci * HEADS:(ci + 1) * HEADS]
        sbs = [s.astype(BF16) for s in state]
        us = [m["u0"] - _dot_nt(m["w"], sb) for m, sb in zip(ms, sbs)]
        qs = [_dot_nt(m["qb"], sb) for m, sb in zip(ms, sbs)]
        pus = [_dot(m["p"], u.astype(BF16)) for m, u in zip(ms, us)]
        state = [m["gend"] * s + _dot_tn((u * m["tail"]).astype(BF16), m["kb"])
                 for m, s, u in zip(ms, state, us)]
        r0 = ci * c
        rows = min(c, tl_in - r0)
        if rows > 0:
            for h, (m, qs_h, pu) in enumerate(zip(ms, qs, pus)):
                hs = slice(h * HW, (h + 1) * HW)
                o = _rms(m["gexp"] * qs_h + pu, onorm_ref[...])[0:rows]
                gin = gin_ref[r0:r0 + rows, hs]
                o_ref[r0:r0 + rows, hs] = (o * (gin * _sigmoid(gin))).astype(o_ref.dtype)
    for h in range(HEADS):
        s_ref[h] = state[h]

    @pl.when(t == pl.num_programs(1) - 1)
    def _():
        sout_ref[...] = s_ref[...]


def gdn(z3, conv_buf, s0, layer, conv_w, a_log, dt_bias, o_norm, tl, tl_in, valid, act_dtype):
    b, l, _ = z3.shape
    nt = l // tl_in
    assert tl % GDN_CHUNK == 0 and (tl_in == tl or nt == 1)
    lane_pad = lambda v, off: jnp.zeros((1, HW), F32).at[0, off:off + HEADS].set(v)
    body = functools.partial(_gdn_body, tl=tl, tl_in=tl_in, valid=valid)
    return pl.pallas_call(
        body,
        grid=(b, nt),
        in_specs=[pl.BlockSpec((None, tl_in, GDN_CONV_CH), lambda i, t: (i, t, _Z_QKV // GDN_CONV_CH)),
                  pl.BlockSpec((None, tl_in, WIDTH), lambda i, t: (i, t, _Z_GIN // WIDTH)),
                  pl.BlockSpec((None, tl_in, HW), lambda i, t: (i, t, _Z_AB // HW)),
                  pl.BlockSpec((None, GDN_CONV - 1, GDN_CONV_CH), lambda i, t: (i, 0, 0)),
                  pl.BlockSpec((None, None, HEADS, HW, HW), lambda i, t: (layer, i, 0, 0, 0)),
                  pl.BlockSpec((GDN_CONV, GDN_CONV_CH), lambda i, t: (0, 0)),
                  pl.BlockSpec((1, HW), lambda i, t: (0, 0)),
                  pl.BlockSpec((1, HW), lambda i, t: (0, 0)),
                  pl.BlockSpec((1, HW), lambda i, t: (0, 0))],
        out_specs=[pl.BlockSpec((None, tl_in, WIDTH), lambda i, t: (i, t, 0)),
                   pl.BlockSpec((None, HEADS, HW, HW), lambda i, t: (i, 0, 0, 0))],
        out_shape=[jax.ShapeDtypeStruct((b, l, WIDTH), act_dtype),
                   jax.ShapeDtypeStruct((b, HEADS, HW, HW), F32)],
        scratch_shapes=[pltpu.VMEM((tl + 8, GDN_CONV_CH), F32),
                        pltpu.VMEM((tl, HW), F32),
                        pltpu.VMEM((HEADS, HW, HW), F32)],
        compiler_params=_params("parallel", "arbitrary"),
        name="gdn",
    )(z3, z3, z3, conv_buf, s0, conv_w, lane_pad(a_log, 0), lane_pad(dt_bias, 0), o_norm.reshape(1, HW))


def _rope_tables(pos):
    half = ROPE_DIM // 2
    inv = ROPE_THETA ** (-jnp.arange(0, ROPE_DIM, 2, dtype=F32) / ROPE_DIM)
    ang = pos.astype(F32)[:, None] * inv[None, :]
    m = jnp.arange(HW) % DIFF_DH
    cos = jnp.cos(ang)[:, m % half]
    sin = jnp.sin(ang)[:, m % half]
    cos_t = jnp.where(m < ROPE_DIM, cos, 1.0)
    sin_a = jnp.where(m < half, -sin, 0.0)
    sin_b = jnp.where((m >= half) & (m < ROPE_DIM), sin, 0.0)
    return cos_t.astype(F32), sin_a.astype(F32), sin_b.astype(F32)


def _rope_body(q_ref, k_ref, v_ref, cos_ref, sa_ref, sb_ref, qm_ref, kr_ref, kb_ref, vf_ref, vb_ref):
    half = ROPE_DIM // 2
    cos, sa, sb = cos_ref[...], sa_ref[...], sb_ref[...]
    lane = lax.broadcasted_iota(jnp.int32, (1, HW), 1)
    first_map = lane < DIFF_DH

    def rot(x):
        return x * cos + pltpu.roll(x, HW - half, 1) * sa + pltpu.roll(x, half, 1) * sb

    for h in range(HEADS):
        hs = slice(h * HW, (h + 1) * HW)
        q = rot(q_ref[:, hs]) * (DIFF_DH ** -0.5)
        qm_ref[0, :, hs] = jnp.where(first_map, q, 0.0).astype(qm_ref.dtype)
        qm_ref[1, :, hs] = jnp.where(first_map, 0.0, q).astype(qm_ref.dtype)
        k = rot(k_ref[:, hs])
        kr_ref[:, h, :] = k
        kb_ref[:, hs] = k.astype(kb_ref.dtype)
        vf_ref[:, h, :] = v_ref[:, hs]
    vb_ref[...] = v_ref[...].astype(vb_ref.dtype)


def rope_split(z, tables, tr, act_dtype):
    t = z.shape[0]
    nl = tables[0].shape[0] // tr
    zspec = lambda off: pl.BlockSpec((tr, WIDTH), lambda i: (i, off // WIDTH))
    tspec = pl.BlockSpec((tr, HW), lambda i: (i % nl, 0))
    ospec = pl.BlockSpec((tr, WIDTH), lambda i: (i, 0))
    hspec = pl.BlockSpec((tr, HEADS, HW), lambda i: (i, 0, 0))
    return pl.pallas_call(
        _rope_body,
        grid=(t // tr,),
        in_specs=[zspec(_Z_DQ), zspec(_Z_DK), zspec(_Z_DV), tspec, tspec, tspec],
        out_specs=[pl.BlockSpec((2, tr, WIDTH), lambda i: (0, i, 0)), hspec, ospec, hspec, ospec],
        out_shape=[jax.ShapeDtypeStruct((2, t, WIDTH), act_dtype),
                   jax.ShapeDtypeStruct((t, HEADS, HW), F32),
                   jax.ShapeDtypeStruct((t, WIDTH), act_dtype),
                   jax.ShapeDtypeStruct((t, HEADS, HW), F32),
                   jax.ShapeDtypeStruct((t, WIDTH), act_dtype)],
        compiler_params=_params("parallel"),
        name="rope_split",
    )(z, z, z, *tables)


def _lambda(lq1_ref, lk1_ref, lq2_ref, lk2_ref, lam_init):
    s1 = jnp.sum(lq1_ref[...] * lk1_ref[...], axis=-1, keepdims=True)
    s2 = jnp.sum(lq2_ref[...] * lk2_ref[...], axis=-1, keepdims=True)
    return jnp.exp(s1) - jnp.exp(s2) + lam_init


_FLASH_ROWS = 256


def _flash_body(qi_ref, ki_ref, q_ref, k_ref, v_ref, lq1_ref, lk1_ref, lq2_ref, lk2_ref, sub_ref, o_ref,
                m_ref, acc_ref, *, lam_init):
    pair = pl.program_id(2)
    qi = qi_ref[pair]
    ki = ki_ref[pair]
    tq = q_ref.shape[1]
    tk = k_ref.shape[0]
    ratio = tq // tk
    rows = min(tq, _FLASH_ROWS)

    @pl.when(ki == 0)
    def _():
        m_ref[...] = jnp.full_like(m_ref, NEG)
        acc_ref[...] = jnp.zeros_like(acc_ref)

    def block(diag):
        k = k_ref[...]
        v = v_ref[...]
        v1 = jnp.concatenate([v, jnp.ones_like(v)], axis=1)
        col0 = 0 if diag is None else diag * tk
        chains = [(c, r0) for c in range(2) for r0 in range(0, tq, rows) if r0 + rows > col0]

        def scores(c, r0):
            s = _dot_nt(q_ref[c, r0:r0 + rows, :], k)
            if diag is not None and r0 < col0 + tk - 1:
                ri = lax.broadcasted_iota(jnp.int32, s.shape, 0) + r0
                ci = lax.broadcasted_iota(jnp.int32, s.shape, 1) + col0
                s = jnp.where(ci <= ri, s, NEG)
            return s

        ahead = 2
        pending = [scores(*ch) for ch in chains[:ahead]]
        for i, (c, r0) in enumerate(chains):
            rs = slice(r0, r0 + rows)
            s = pending.pop(0)
            tiles = [s[:, j * HW:(j + 1) * HW] for j in range(tk // HW)]
            tile_max = functools.reduce(jnp.maximum, tiles)
            m_prev = m_ref[c, rs, :]
            m_new = jnp.maximum(m_prev, jnp.max(tile_max, axis=-1, keepdims=True))
            alpha = jnp.exp(m_prev - m_new)
            p = jnp.concatenate([jnp.exp(tl - m_new).astype(BF16) for tl in tiles], axis=1)
            pv = _dot(p, v1)
            if i + ahead < len(chains):
                pending.append(scores(*chains[i + ahead]))
            acc_ref[c, rs, :] = jnp.concatenate([alpha, alpha], axis=1) * acc_ref[c, rs, :] + pv
            m_ref[c, rs, :] = m_new

    pl.when(ki < qi * ratio)(functools.partial(block, None))
    for d in range(ratio):
        pl.when(ki == qi * ratio + d)(functools.partial(block, d))

    @pl.when(ki == qi * ratio + ratio - 1)
    def _():
        lam = _lambda(lq1_ref, lk1_ref, lq2_ref, lk2_ref, lam_init)
        o = (acc_ref[0, :, 0:HW] * (1.0 / acc_ref[0, :, HW:2 * HW])
             - lam * (acc_ref[1, :, 0:HW] * (1.0 / acc_ref[1, :, HW:2 * HW])))
        o_ref[...] = (_rms(o, sub_ref[...]) * (1.0 - lam_init)).astype(o_ref.dtype)


def flash_diff(qm, kb, vb, lparams, subln, lam_init, batch, tq, tk):
    t = kb.shape[0]
    l = t // batch
    nq = l // tq
    nk = l // tk
    assert tq % tk == 0
    pairs = [(i, j) for i in range(nq) for j in range((i + 1) * (tq // tk))]
    qi_arr = jnp.asarray([p[0] for p in pairs], jnp.int32)
    ki_arr = jnp.asarray([p[1] for p in pairs], jnp.int32)
    lspec = pl.BlockSpec((1, DIFF_DH), lambda b, h, p, qi, ki: (0, 0))
    kvspec = pl.BlockSpec((tk, HW), lambda b, h, p, qi, ki: (b * nk + ki[p], h))
    grid_spec = pltpu.PrefetchScalarGridSpec(
        num_scalar_prefetch=2,
        grid=(batch, HEADS, len(pairs)),
        in_specs=[pl.BlockSpec((2, tq, HW), lambda b, h, p, qi, ki: (0, b * nq + qi[p], h)),
                  kvspec, kvspec, lspec, lspec, lspec, lspec,
                  pl.BlockSpec((1, HW), lambda b, h, p, qi, ki: (0, 0))],
        out_specs=pl.BlockSpec((tq, HW), lambda b, h, p, qi, ki: (b * nq + qi[p], h)),
        scratch_shapes=[pltpu.VMEM((2, tq, HW), F32), pltpu.VMEM((2, tq, 2 * HW), F32)],
    )
    return pl.pallas_call(
        functools.partial(_flash_body, lam_init=lam_init),
        grid_spec=grid_spec,
        out_shape=jax.ShapeDtypeStruct((t, WIDTH), BF16),
        compiler_params=_params("parallel", "parallel", "arbitrary"),
        name="flash_diff",
    )(qi_arr, ki_arr, qm, kb, vb, *[p.reshape(1, DIFF_DH) for p in lparams], subln.reshape(1, HW))


_PAGES_PER_STEP = 4


def _paged_body(pt_ref, q_ref, *refs, lam_init, n_valid):
    del pt_ref
    pp = _PAGES_PER_STEP
    k_refs, v_refs = refs[:pp], refs[pp:2 * pp]
    (kn_ref, vn_ref, lq1_ref, lk1_ref, lq2_ref, lk2_ref, sub_ref, o_ref,
     m_ref, l_ref, acc_ref, kn_scr, vn_scr) = refs[2 * pp:]
    p_idx = pl.program_id(1)
    nq = q_ref.shape[1]

    @pl.when(p_idx == 0)
    def _():
        m_ref[...] = jnp.full_like(m_ref, NEG)
        l_ref[...] = jnp.zeros_like(l_ref)
        acc_ref[...] = jnp.zeros_like(acc_ref)

    def head_q(h):
        hs = slice(h * HW, (h + 1) * HW)
        return jnp.concatenate([q_ref[0, :, hs], q_ref[1, :, hs]], axis=0).astype(BF16)

    def update(k_page, v_page, n_pages, mask):
        for h in range(HEADS):
            rs = slice(h * 2 * nq, (h + 1) * 2 * nq)
            q2 = head_q(h)
            s = jnp.concatenate([_dot_nt(q2, k_page(j, h).astype(BF16)) for j in range(n_pages)], axis=-1)
            if mask is not None:
                s = jnp.where(mask, s, NEG)
            m_prev = m_ref[rs]
            m_new = jnp.maximum(m_prev, jnp.max(s, axis=-1, keepdims=True))
            alpha = jnp.exp(m_prev - m_new)
            p = jnp.exp(s - m_new).astype(BF16)
            l_ref[rs] = alpha * l_ref[rs] + jnp.sum(p.astype(F32), axis=-1, keepdims=True)
            pv = _dot(p[:, 0:PAGE], v_page(0, h).astype(BF16))
            for j in range(1, n_pages):
                pv = pv + _dot(p[:, j * PAGE:(j + 1) * PAGE], v_page(j, h).astype(BF16))
            acc_ref[rs] = alpha * acc_ref[rs] + pv
            m_ref[rs] = m_new

    update(lambda j, h: k_refs[j][:, h, :], lambda j, h: v_refs[j][:, h, :], pp, None)

    @pl.when(p_idx == pl.num_programs(1) - 1)
    def _():
        kn_scr[...] = jnp.zeros_like(kn_scr)
        vn_scr[...] = jnp.zeros_like(vn_scr)
        kn_scr[0:nq, :] = kn_ref[...]
        vn_scr[0:nq, :] = vn_ref[...]
        ri = lax.broadcasted_iota(jnp.int32, (2 * nq, PAGE), 0) % nq
        ci = lax.broadcasted_iota(jnp.int32, (2 * nq, PAGE), 1)
        update(lambda j, h: kn_scr[:, h * HW:(h + 1) * HW], lambda j, h: vn_scr[:, h * HW:(h + 1) * HW], 1,
               (ci <= ri) & (ci < n_valid))
        lam = _lambda(lq1_ref, lk1_ref, lq2_ref, lk2_ref, lam_init)
        for h in range(HEADS):
            hs = slice(h * HW, (h + 1) * HW)
            r0 = slice(h * 2 * nq, h * 2 * nq + nq)
            r1 = slice(h * 2 * nq + nq, (h + 1) * 2 * nq)
            o = acc_ref[r0] * (1.0 / l_ref[r0]) - lam * (acc_ref[r1] * (1.0 / l_ref[r1]))
            o_ref[:, hs] = (_rms(o, sub_ref[...]) * (1.0 - lam_init)).astype(o_ref.dtype)


def paged_diff(qm, kb, vb, cache_k, cache_v, page_table, layer, lparams, subln, lam_init, n_valid):
    _, b, nq, _ = qm.shape
    n_pages = page_table.shape[1]
    pp = _PAGES_PER_STEP
    assert n_pages % pp == 0
    page_spec = lambda j: pl.BlockSpec((None, None, PAGE, HEADS, HW),
                                       lambda i, p, pt: (layer, pt[i, p * pp + j], 0, 0, 0))
    new_spec = pl.BlockSpec((None, nq, WIDTH), lambda i, p, pt: (i, 0, 0))
    lspec = pl.BlockSpec((1, DIFF_DH), lambda i, p, pt: (0, 0))
    grid_spec = pltpu.PrefetchScalarGridSpec(
        num_scalar_prefetch=1,
        grid=(b, n_pages // pp),
        in_specs=([pl.BlockSpec((2, None, nq, WIDTH), lambda i, p, pt: (0, i, 0, 0))]
                  + [page_spec(j) for j in range(pp)] + [page_spec(j) for j in range(pp)]
                  + [new_spec, new_spec, lspec, lspec, lspec, lspec,
                     pl.BlockSpec((1, HW), lambda i, p, pt: (0, 0))]),
        out_specs=pl.BlockSpec((None, nq, WIDTH), lambda i, p, pt: (i, 0, 0)),
        scratch_shapes=[pltpu.VMEM((HEADS * 2 * nq, 1), F32), pltpu.VMEM((HEADS * 2 * nq, 1), F32),
                        pltpu.VMEM((HEADS * 2 * nq, HW), F32),
                        pltpu.VMEM((PAGE, WIDTH), F32), pltpu.VMEM((PAGE, WIDTH), F32)],
    )
    return pl.pallas_call(
        functools.partial(_paged_body, lam_init=lam_init, n_valid=n_valid),
        grid_spec=grid_spec,
        out_shape=jax.ShapeDtypeStruct((b, nq, WIDTH), F32),
        compiler_params=_params("parallel", "arbitrary"),
        name="paged_diff",
    )(page_table, qm, *([cache_k] * pp), *([cache_v] * pp), kb, vb,
      *[p.reshape(1, DIFF_DH) for p in lparams], subln.reshape(1, HW))


def _gelu(x):
    return 0.5 * x * (1.0 + jnp.tanh(math.sqrt(2.0 / math.pi) * (x + 0.044715 * (x * x * x))))


def _gmlp_body(u_ref, v_ref, lg_ref, lb_ref, ws_ref, bst_ref, o_ref, gv_ref, v_scr, *, tr):
    u = _gelu(u_ref[...])
    v = _gelu(v_ref[...])
    mu = jnp.mean(v, axis=-1, keepdims=True)
    vc = v - mu
    v = vc * lax.rsqrt(jnp.mean(vc * vc, axis=-1, keepdims=True) + EPS) * lg_ref[...] + lb_ref[...]
    gv_ref[...] = v
    if tr < GMLP_CHUNK:
        v_scr[...] = jnp.zeros_like(v_scr)
        v_scr[0:tr, :] = v
    ii = lax.broadcasted_iota(jnp.int32, (tr, GMLP_CHUNK), 0)
    jj = lax.broadcasted_iota(jnp.int32, (tr, GMLP_CHUNK), 1)
    for g in range(HEADS):
        gs = slice(g * HW, (g + 1) * HW)
        w = jnp.where(ii >= jj, ws_ref[g, 0:tr, :], 0.0).astype(BF16)
        vg = (v_scr[:, gs] if tr < GMLP_CHUNK else v[:, gs]).astype(BF16)
        mixed = _dot(w, vg) + bst_ref[0:tr, g:g + 1]
        o_ref[:, gs] = (u[:, gs] * mixed).astype(o_ref.dtype)


def gmlp(z, ln_g, ln_b, ws, bs, tr, act_dtype):
    t = z.shape[0]
    zspec = lambda off: pl.BlockSpec((tr, WIDTH), lambda i: (i, off // WIDTH))
    ospec = pl.BlockSpec((tr, WIDTH), lambda i: (i, 0))
    vec = pl.BlockSpec((1, WIDTH), lambda i: (0, 0))
    return pl.pallas_call(
        functools.partial(_gmlp_body, tr=tr),
        grid=(t // tr,),
        in_specs=[zspec(_Z_CU), zspec(_Z_CV), vec, vec,
                  pl.BlockSpec((HEADS, GMLP_CHUNK, GMLP_CHUNK), lambda i: (0, 0, 0)),
                  pl.BlockSpec((GMLP_CHUNK, HEADS), lambda i: (0, 0))],
        out_specs=[ospec, ospec],
        out_shape=[jax.ShapeDtypeStruct((t, WIDTH), act_dtype), jax.ShapeDtypeStruct((t, WIDTH), F32)],
        scratch_shapes=[pltpu.VMEM((GMLP_CHUNK, WIDTH), F32)],
        compiler_params=_params("parallel"),
        name="gmlp",
    )(z, z, ln_g.reshape(1, WIDTH), ln_b.reshape(1, WIDTH), ws, bs.T)


def _merge_body(x_ref, gate_ref, oa_ref, ob_ref, oc_ref, wa_ref, wb_ref, wc_ref, wo_ref, o_ref):
    d = D_MODEL
    merged = (_sigmoid(gate_ref[:, 0:d]) * _dot(oa_ref[...].astype(BF16), wa_ref[...])
              + _sigmoid(gate_ref[:, d:2 * d]) * _dot(ob_ref[...].astype(BF16), wb_ref[...])
              + _sigmoid(gate_ref[:, 2 * d:3 * d]) * _dot(oc_ref[...].astype(BF16), wc_ref[...]))
    o_ref[...] = x_ref[...] + _dot(merged.astype(BF16), wo_ref[...])


def merge(x, z, oa, ob, oc, wa, wb, wc, wo, tm):
    t, d = x.shape
    row = lambda w: pl.BlockSpec((tm, w), lambda i: (i, 0))
    full = lambda a: pl.BlockSpec(a.shape, lambda i: (0, 0))
    return pl.pallas_call(
        _merge_body,
        grid=(t // tm,),
        in_specs=[row(d), row(3 * d), row(WIDTH), row(WIDTH), row(WIDTH),
                  full(wa), full(wb), full(wc), full(wo)],
        out_specs=row(d),
        out_shape=jax.ShapeDtypeStruct((t, d), F32),
        compiler_params=_params("parallel"),
        name="merge",
    )(x, z, oa, ob, oc, wa, wb, wc, wo)


def _xattn_body(x_ref, g_ref, wq_ref, mk_ref, mv_ref, wo_ref, o_ref):
    x = x_ref[...]
    q = _dot(_rms(x, g_ref[...]).astype(BF16), wq_ref[...])
    if len(mk_ref.shape) == 3:
        head = lambda ref, h: ref[:, h, :]
    else:
        head = lambda ref, h: ref[:, h * HW:(h + 1) * HW]
    outs = []
    for h in range(HEADS):
        hs = slice(h * HW, (h + 1) * HW)
        s = _dot_nt(q[:, hs].astype(BF16), head(mk_ref, h).astype(BF16)) * (HW ** -0.5)
        e = jnp.exp(s - jnp.max(s, axis=-1, keepdims=True))
        p = e * (1.0 / jnp.sum(e, axis=-1, keepdims=True))
        outs.append(_dot(p.astype(BF16), head(mv_ref, h).astype(BF16)))
    o = jnp.concatenate(outs, axis=-1)
    o_ref[...] = x + _dot(o.astype(BF16), wo_ref[...])


def xattn(x, g, wq, mem_k, mem_v, layer, wo, tm, tiles_per_seq):
    t, d = x.shape
    m = mem_k.shape[2]
    tail = mem_k.shape[3:]
    mem_spec = pl.BlockSpec((None, None, m) + tail,
                            lambda i: (layer, i // tiles_per_seq) + (0,) * (1 + len(tail)))
    return pl.pallas_call(
        _xattn_body,
        grid=(t // tm,),
        in_specs=[pl.BlockSpec((tm, d), lambda i: (i, 0)),
                  pl.BlockSpec((1, d), lambda i: (0, 0)),
                  pl.BlockSpec(wq.shape, lambda i: (0, 0)),
                  mem_spec, mem_spec,
                  pl.BlockSpec(wo.shape, lambda i: (0, 0))],
        out_specs=pl.BlockSpec((tm, d), lambda i: (i, 0)),
        out_shape=jax.ShapeDtypeStruct((t, d), F32),
        compiler_params=_params("parallel"),
        name="xattn",
    )(x, g.reshape(1, d), wq, mem_k, mem_v, wo)


def _ffn_body(x_ref, g_ref, w1_ref, w2_ref, o_ref, h_ref):
    @pl.when(pl.program_id(1) == 0)
    def _():
        x = x_ref[...]
        h_ref[...] = _rms(x, g_ref[...]).astype(BF16)
        o_ref[...] = x

    a = jnp.maximum(_dot(h_ref[...], w1_ref[...]), 0.0)
    o_ref[...] += _dot((a * a).astype(BF16), w2_ref[...])


def ffn(x, g, w1, w2, tm, tf):
    t, d = x.shape
    f = w1.shape[1]
    return pl.pallas_call(
        _ffn_body,
        grid=(t // tm, f // tf),
        in_specs=[pl.BlockSpec((tm, d), lambda i, j: (i, 0)),
                  pl.BlockSpec((1, d), lambda i, j: (0, 0)),
                  pl.BlockSpec((d, tf), lambda i, j: (0, j)),
                  pl.BlockSpec((tf, d), lambda i, j: (j, 0))],
        out_specs=pl.BlockSpec((tm, d), lambda i, j: (i, 0)),
        out_shape=jax.ShapeDtypeStruct((t, d), F32),
        scratch_shapes=[pltpu.VMEM((tm, d), BF16)],
        compiler_params=_params("parallel", "arbitrary"),
        name="ffn",
    )(x, g.reshape(1, d), w1, w2)


def _final_norm_body(x_ref, g_ref, o_ref):
    o_ref[...] = _rms(x_ref[...], g_ref[...])


def final_norm(x, g, tm):
    t, d = x.shape
    return pl.pallas_call(
        _final_norm_body,
        grid=(t // tm,),
        in_specs=[pl.BlockSpec((tm, d), lambda i: (i, 0)), pl.BlockSpec((1, d), lambda i: (0, 0))],
        out_specs=pl.BlockSpec((tm, d), lambda i: (i, 0)),
        out_shape=jax.ShapeDtypeStruct((t, d), F32),
        compiler_params=_params("parallel"),
        name="final_norm",
    )(x, g.reshape(1, d))


def _regroup_w_in(w):
    o = 0
    parts = {}
    for name, size in (("q", WIDTH), ("k", WIDTH), ("v", WIDTH), ("a", HEADS), ("b", HEADS), ("g", WIDTH),
                       ("dq", WIDTH), ("dk", WIDTH), ("dv", WIDTH), ("cu", WIDTH), ("cv", WIDTH),
                       ("gates", 3 * D_MODEL)):
        parts[name] = w[:, o:o + size]
        o += size
    used = _Z_AB + 2 * HEADS
    cols = [parts[n] for n in ("gates", "q", "k", "v", "g", "dq", "dk", "dv", "cu", "cv", "a", "b")]
    cols.append(jnp.zeros((w.shape[0], _Z_COLS - used), w.dtype))
    return jnp.concatenate(cols, axis=1).astype(BF16)


def _tile(n, want):
    while n % want:
        want //= 2
    return want


def kernel(x_prompt, x_sample, cache_diff_k, cache_diff_v, page_table, cache_mem_k, cache_mem_v, state_gdn, state_gdn_conv, mem_prompt, w_in, norm_mix, gdn_conv_w, gdn_A_log, gdn_dt_bias, gdn_o_norm, diff_lq1, diff_lk1, diff_lq2, diff_lk2, diff_subln, gmlp_ln_g, gmlp_ln_b, gmlp_ws, gmlp_b, w_br_gdn, w_br_diff, w_br_gmlp, w_out, norm_xa, norm_mem, xa_wq, xa_wk, xa_wv, xa_wo, norm_ffn, ffn_w1, ffn_w2, norm_final):
    depth = w_in.shape[0]
    bp, lp, d = x_prompt.shape
    bs, ls, _ = x_sample.shape
    mem_len = mem_prompt.shape[1]
    past_len = page_table.shape[1] * PAGE
    lpad = SAMPLE_PAD
    bf = lambda a: a.astype(BF16)

    mem_tok = mem_prompt.reshape(bp * mem_len, d)
    mem_kv = [norm_matmul(mem_tok, norm_mem[l], bf(jnp.concatenate([xa_wk[l], xa_wv[l]], axis=1)),
                          _tile(bp * mem_len, 512), 512) for l in range(depth)]
    mem_k_p = jnp.stack([kv[:, :WIDTH] for kv in mem_kv]).reshape(depth, bp, mem_len, WIDTH)
    mem_v_p = jnp.stack([kv[:, WIDTH:] for kv in mem_kv]).reshape(depth, bp, mem_len, WIDTH)

    weights = [dict(w_in=_regroup_w_in(w_in[l]), wa=bf(w_br_gdn[l]), wb=bf(w_br_diff[l]), wc=bf(w_br_gmlp[l]),
                    wo=bf(w_out[l]), xq=bf(xa_wq[l]), xo=bf(xa_wo[l]), w1=bf(ffn_w1[l]), w2=bf(ffn_w2[l]))
               for l in range(depth)]

    def trunk(x, b, l, l_valid, pos, mem_k, mem_v, s_in, buf_in, paged):
        t = b * l
        prompt = paged is None
        act = BF16 if prompt else F32
        tm = _tile(t, 1024)
        tables = _rope_tables(pos)
        outs = dict(k=[], v=[], s=[], buf=[], gv=[])
        for li in range(depth):
            w = weights[li]
            lam_init = 0.8 - 0.6 * math.exp(-0.3 * li)
            lparams = (diff_lq1[li], diff_lk1[li], diff_lq2[li], diff_lk2[li])
            z = norm_matmul(x, norm_mix[li], w["w_in"], tm, 1024)
            z3 = z.reshape(b, l, _Z_COLS)
            o_a, s_new = gdn(z3, buf_in[li], s_in, li, gdn_conv_w[li], gdn_A_log[li], gdn_dt_bias[li],
                             gdn_o_norm[li], _tile(l, 256) if prompt else GDN_CHUNK, _tile(l, 256) if prompt else l,
                             l_valid, act)
            outs["buf"].append(z3[:, l_valid - (GDN_CONV - 1):l_valid, _Z_QKV:_Z_QKV + GDN_CONV_CH])
            qm, kr, kb, vf, vb = rope_split(z, tables, _tile(l, 512), act)
            if prompt:
                o_b = flash_diff(qm, kb, vb, lparams, diff_subln[li], lam_init, b, _tile(l, 1024), _tile(l, 512))
            else:
                ck, cv, pt = paged
                o_b = paged_diff(qm.reshape(2, b, l, WIDTH), kb.reshape(b, l, WIDTH), vb.reshape(b, l, WIDTH),
                                 ck, cv, pt, li, lparams, diff_subln[li], lam_init, l_valid)
                o_b = o_b.reshape(t, WIDTH)
            o_c, gv = gmlp(z, gmlp_ln_g[li], gmlp_ln_b[li], gmlp_ws[li], gmlp_b[li], min(l, GMLP_CHUNK), act)
            x = merge(x, z, o_a.reshape(t, WIDTH), o_b, o_c, w["wa"], w["wb"], w["wc"], w["wo"], _tile(t, 512))
            xa_tm = _tile(l, 512)
            x = xattn(x, norm_xa[li], w["xq"], mem_k, mem_v, li, w["xo"], xa_tm, l // xa_tm)
            x = ffn(x, norm_ffn[li], w["w1"], w["w2"], tm, 1024)
            outs["k"].append(kr)
            outs["v"].append(vf)
            outs["s"].append(s_new)
            outs["gv"].append(gv)
        y = final_norm(x, norm_final, tm)
        seq = lambda a: a.reshape((depth, b, l) + a.shape[2:])[:, :, :l_valid]
        return (y.reshape(b, l, d)[:, :l_valid], seq(jnp.stack(outs["k"])), seq(jnp.stack(outs["v"])),
                jnp.stack(outs["s"]), jnp.stack(outs["buf"]), seq(jnp.stack(outs["gv"])))

    s0_p = jnp.zeros((depth, bp, HEADS, HW, HW), F32)
    buf0_p = jnp.zeros((depth, bp, GDN_CONV - 1, GDN_CONV_CH), F32)
    (y_p, k_p, v_p, s_p, buf_p, _) = trunk(x_prompt.reshape(bp * lp, d), bp, lp, lp, jnp.arange(lp),
                                           mem_k_p, mem_v_p, s0_p, buf0_p, None)

    x_s = jnp.pad(x_sample, ((0, 0), (0, lpad - ls), (0, 0))).reshape(bs * lpad, d)
    (y_s, k_s, v_s, s_s, buf_s, gv_s) = trunk(x_s, bs, lpad, ls, past_len + jnp.arange(lpad),
                                              cache_mem_k, cache_mem_v, state_gdn, state_gdn_conv,
                                              (cache_diff_k, cache_diff_v, page_table))

    return (y_p, y_s, k_p, v_p,
            mem_k_p.reshape(depth, bp, mem_len, HEADS, HW), mem_v_p.reshape(depth, bp, mem_len, HEADS, HW),
            s_p, buf_p, k_s, v_s, s_s, buf_s, gv_s)
```

```python
import functools
import math

import jax
import jax.numpy as jnp
from jax import lax
from jax.experimental import pallas as pl
from jax.experimental.pallas import tpu as pltpu

F32 = jnp.float32
BF16 = jnp.bfloat16
EPS = 1e-6
NEG = -1e30

D_MODEL = 1024
HEADS = 4
HW = 128
WIDTH = HEADS * HW
DIFF_DH = 64
ROPE_DIM = 16
ROPE_THETA = 500000.0
GDN_CONV = 4
GDN_CHUNK = 64
GDN_CONV_CH = 3 * WIDTH
GMLP_CHUNK = 128
PAGE = 128
SAMPLE_PAD = 8

_Z_GATES = 0
_Z_QKV = 3 * D_MODEL
_Z_GIN = _Z_QKV + GDN_CONV_CH
_Z_DQ = _Z_GIN + WIDTH
_Z_DK = _Z_DQ + WIDTH
_Z_DV = _Z_DK + WIDTH
_Z_CU = _Z_DV + WIDTH
_Z_CV = _Z_CU + WIDTH
_Z_AB = _Z_CV + WIDTH
_Z_COLS = 8192

_VMEM_LIMIT = 56 * 1024 * 1024


def _params(*sem):
    return pltpu.CompilerParams(dimension_semantics=sem, vmem_limit_bytes=_VMEM_LIMIT)


def _rms(x, g):
    return x * lax.rsqrt(jnp.mean(x * x, axis=-1, keepdims=True) + EPS) * g


def _sigmoid(x):
    return 1.0 / (1.0 + jnp.exp(-x))


def _dot(a, b):
    return jnp.dot(a, b, preferred_element_type=F32)


def _dot_nt(a, b):
    return lax.dot_general(a, b, (((1,), (1,)), ((), ())), preferred_element_type=F32)


def _dot_tn(a, b):
    return lax.dot_general(a, b, (((0,), (0,)), ((), ())), preferred_element_type=F32)


def _split(a):
    hi = a.astype(BF16).astype(F32)
    return hi, a - hi


def _hi_lhs(a):
    hi, lo = _split(a)
    return jnp.concatenate([hi, hi, lo], axis=1).astype(BF16)


def _hi_rhs(b):
    hi, lo = _split(b)
    return jnp.concatenate([hi, lo, hi], axis=0).astype(BF16)


def _dot_hi(a, b):
    return _dot(_hi_lhs(a), _hi_rhs(b))


def _dot_hi_exact_lhs(a_bf16, b):
    hi, lo = _split(b)
    lo_hi, lo_lo = _split(lo)
    return _dot(jnp.concatenate([a_bf16] * 3, axis=1), jnp.concatenate([hi, lo_hi, lo_lo], axis=0).astype(BF16))


def _dot_hi_exact_rhs(a, b_bf16):
    hi, lo = _split(a)
    lo_hi, lo_lo = _split(lo)
    return _dot(jnp.concatenate([hi, lo_hi, lo_lo], axis=1).astype(BF16), jnp.concatenate([b_bf16] * 3, axis=0))


def _norm_mm_body(x_ref, g_ref, w_ref, o_ref, h_ref):
    @pl.when(pl.program_id(1) == 0)
    def _():
        h_ref[...] = _rms(x_ref[...], g_ref[...]).astype(BF16)

    o_ref[...] = _dot(h_ref[...], w_ref[...]).astype(o_ref.dtype)


def norm_matmul(x, g, w, tm, tn):
    t, d = x.shape
    n = w.shape[1]
    return pl.pallas_call(
        _norm_mm_body,
        grid=(t // tm, n // tn),
        in_specs=[pl.BlockSpec((tm, d), lambda i, j: (i, 0)),
                  pl.BlockSpec((1, d), lambda i, j: (0, 0)),
                  pl.BlockSpec((d, tn), lambda i, j: (0, j))],
        out_specs=pl.BlockSpec((tm, tn), lambda i, j: (i, j)),
        out_shape=jax.ShapeDtypeStruct((t, n), F32),
        scratch_shapes=[pltpu.VMEM((tm, d), BF16)],
        compiler_params=_params("parallel", "arbitrary"),
        name="norm_matmul",
    )(x, g.reshape(1, d), w)


def _gdn_body(qkv_ref, gin_ref, ab_ref, buf_ref, s0_ref, cw_ref, alog_ref, dtb_ref, onorm_ref,
              o_ref, sout_ref, xs_ref, ab_scr, s_ref, *, tl, tl_in, valid):
    c = GDN_CHUNK
    nc = tl // c
    t = pl.program_id(1)

    @pl.when(t == 0)
    def _():
        s_ref[...] = s0_ref[...]
        xs_ref[...] = jnp.zeros_like(xs_ref)
        xs_ref[5:8, :] = buf_ref[...]
        ab_scr[...] = jnp.zeros_like(ab_scr)

    xs_ref[8:8 + tl_in, :] = qkv_ref[...]
    ab_scr[0:tl_in, :] = ab_ref[...]
    w = cw_ref[...]
    y = (xs_ref[5:5 + tl, :] * w[0:1] + xs_ref[6:6 + tl, :] * w[1:2]
         + xs_ref[7:7 + tl, :] * w[2:3] + xs_ref[8:8 + tl, :] * w[3:4])
    xs_ref[5:8, :] = xs_ref[tl + 5:tl + 8, :]
    y = y * _sigmoid(y)

    ab = ab_scr[...]
    sp_in = ab + dtb_ref[...]
    softplus = jnp.maximum(sp_in, 0.0) + jnp.log1p(jnp.exp(-jnp.abs(sp_in)))
    g = -jnp.exp(alog_ref[...]) * softplus
    beta = _sigmoid(ab)
    if valid < tl:
        live = lax.broadcasted_iota(jnp.int32, (tl, 1), 0) < valid
        g = jnp.where(live, g, 0.0)
        beta = jnp.where(live, beta, 0.0)

    ti = lax.broadcasted_iota(jnp.int32, (tl, tl), 0)
    tj = lax.broadcasted_iota(jnp.int32, (tl, tl), 1)
    same_chunk = (ti // c) == (tj // c)
    gcol = _dot_hi_exact_lhs(jnp.where(same_chunk & (ti >= tj), 1.0, 0.0).astype(BF16), g)
    grow = _dot_hi_exact_rhs(g.T[0:8, :], jnp.where(same_chunk & (ti <= tj), 1.0, 0.0).astype(BF16))

    ii = lax.broadcasted_iota(jnp.int32, (c, c), 0)
    jj = lax.broadcasted_iota(jnp.int32, (c, c), 1)
    incl = ii >= jj
    strict = ii > jj
    eye = (ii == jj).astype(F32)

    def prepare(ci, h):
        rs = slice(ci * c, (ci + 1) * c)
        gi = gcol[rs, h:h + 1]
        gj = grow[h:h + 1, rs]
        dec_incl = jnp.exp(jnp.where(incl, gi - gj, NEG))
        glast = gcol[(ci + 1) * c - 1:(ci + 1) * c, h:h + 1]
        qh = y[rs, h * HW:(h + 1) * HW]
        kh = y[rs, WIDTH + h * HW:WIDTH + (h + 1) * HW]
        qh = qh * lax.rsqrt(jnp.sum(qh * qh, axis=-1, keepdims=True) + EPS) * (HW ** -0.5)
        kh = kh * lax.rsqrt(jnp.sum(kh * kh, axis=-1, keepdims=True) + EPS)
        return dict(dec_incl=dec_incl, dec_strict=jnp.where(strict, dec_incl, 0.0), gexp=jnp.exp(gi),
                    bcol=beta[rs, 4 + h:5 + h], kh=kh, vh=y[rs, 2 * WIDTH + h * HW:2 * WIDTH + (h + 1) * HW],
                    qb=qh.astype(BF16), kb=kh.astype(BF16), tail=jnp.exp(glast - gi), gend=jnp.exp(glast))

    terms = [prepare(ci, h) for ci in range(nc) for h in range(HEADS)]
    for m in terms:
        kk = _dot_nt(m["kb"], m["kb"])
        m["p"] = (_dot_nt(m["qb"], m["kb"]) * m["dec_incl"]).astype(BF16)
        m["x"] = -(m["bcol"] * kk * m["dec_strict"])
        m["tinv"] = eye + m["x"]
    for m in terms:
        m["x"] = _dot_hi(m["x"], m["x"])
    for _ in range(int(math.log2(c)) - 2):
        for m in terms:
            r = _dot_hi(jnp.concatenate([m["x"], m["tinv"]], axis=0), m["x"])
            m["x"] = r[0:c]
            m["tinv"] = m["tinv"] + r[c:2 * c]
    for m in terms:
        m["tinv"] = m["tinv"] + _dot_hi(m["tinv"], m["x"])
    for m in terms:
        sol = _dot_hi(m["tinv"], jnp.concatenate([m["bcol"] * m["vh"], (m["bcol"] * m["gexp"]) * m["kh"]], axis=-1))
        m["u0"] = sol[:, :HW]
        m["w"] = sol[:, HW:].astype(BF16)

    state = [s_ref[h] for h in range(HEADS)]
    for ci in range(nc):
        ms = terms[ci * HEADS:(ci + 1) * HEADS]
        sbs = [s.astype(BF16) for s in state]
        us = [m["u0"] - _dot_nt(m["w"], sb) for m, sb in zip(ms, sbs)]
        qs = [_dot_nt(m["qb"], sb) for m, sb in zip(ms, sbs)]
        pus = [_dot(m["p"], u.astype(BF16)) for m, u in zip(ms, us)]
        state = [m["gend"] * s + _dot_tn((u * m["tail"]).astype(BF16), m["kb"])
                 for m, s, u in zip(ms, state, us)]
        r0 = ci * c
        rows = min(c, tl_in - r0)
        if rows > 0:
            for h, (m, qs_h, pu) in enumerate(zip(ms, qs, pus)):
                hs = slice(h * HW, (h + 1) * HW)
                o = _rms(m["gexp"] * qs_h + pu, onorm_ref[...])[0:rows]
                gin = gin_ref[r0:r0 + rows, hs]
                o_ref[r0:r0 + rows, hs] = (o * (gin * _sigmoid(gin))).astype(o_ref.dtype)
    for h in range(HEADS):
        s_ref[h] = state[h]

    @pl.when(t == pl.num_programs(1) - 1)
    def _():
        sout_ref[...] = s_ref[...]


def gdn(z3, conv_buf, s0, layer, conv_w, a_log, dt_bias, o_norm, tl, tl_in, valid, act_dtype):
    b, l, _ = z3.shape
    nt = l // tl_in
    assert tl % GDN_CHUNK == 0 and (tl_in == tl or nt == 1)
    lane_pad = lambda v, off: jnp.zeros((1, HW), F32).at[0, off:off + HEADS].set(v)
    body = functools.partial(_gdn_body, tl=tl, tl_in=tl_in, valid=valid)
    return pl.pallas_call(
        body,
        grid=(b, nt),
        in_specs=[pl.BlockSpec((None, tl_in, GDN_CONV_CH), lambda i, t: (i, t, _Z_QKV // GDN_CONV_CH)),
                  pl.BlockSpec((None, tl_in, WIDTH), lambda i, t: (i, t, _Z_GIN // WIDTH)),
                  pl.BlockSpec((None, tl_in, HW), lambda i, t: (i, t, _Z_AB // HW)),
                  pl.BlockSpec((None, GDN_CONV - 1, GDN_CONV_CH), lambda i, t: (i, 0, 0)),
                  pl.BlockSpec((None, None, HEADS, HW, HW), lambda i, t: (layer, i, 0, 0, 0)),
                  pl.BlockSpec((GDN_CONV, GDN_CONV_CH), lambda i, t: (0, 0)),
                  pl.BlockSpec((1, HW), lambda i, t: (0, 0)),
                  pl.BlockSpec((1, HW), lambda i, t: (0, 0)),
                  pl.BlockSpec((1, HW), lambda i, t: (0, 0))],
        out_specs=[pl.BlockSpec((None, tl_in, WIDTH), lambda i, t: (i, t, 0)),
                   pl.BlockSpec((None, HEADS, HW, HW), lambda i, t: (i, 0, 0, 0))],
        out_shape=[jax.ShapeDtypeStruct((b, l, WIDTH), act_dtype),
                   jax.ShapeDtypeStruct((b, HEADS, HW, HW), F32)],
        scratch_shapes=[pltpu.VMEM((tl + 8, GDN_CONV_CH), F32),
                        pltpu.VMEM((tl, HW), F32),
                        pltpu.VMEM((HEADS, HW, HW), F32)],
        compiler_params=_params("parallel", "arbitrary"),
        name="gdn",
    )(z3, z3, z3, conv_buf, s0, conv_w, lane_pad(a_log, 0), lane_pad(dt_bias, 0), o_norm.reshape(1, HW))


def _rope_tables(pos):
    half = ROPE_DIM // 2
    inv = ROPE_THETA ** (-jnp.arange(0, ROPE_DIM, 2, dtype=F32) / ROPE_DIM)
    ang = pos.astype(F32)[:, None] * inv[None, :]
    m = jnp.arange(HW) % DIFF_DH
    cos = jnp.cos(ang)[:, m % half]
    sin = jnp.sin(ang)[:, m % half]
    cos_t = jnp.where(m < ROPE_DIM, cos, 1.0)
    sin_a = jnp.where(m < half, -sin, 0.0)
    sin_b = jnp.where((m >= half) & (m < ROPE_DIM), sin, 0.0)
    return cos_t.astype(F32), sin_a.astype(F32), sin_b.astype(F32)


def _rope_body(q_ref, k_ref, v_ref, cos_ref, sa_ref, sb_ref, qm_ref, kr_ref, kb_ref, vf_ref, vb_ref):
    half = ROPE_DIM // 2
    cos, sa, sb = cos_ref[...], sa_ref[...], sb_ref[...]
    lane = lax.broadcasted_iota(jnp.int32, (1, HW), 1)
    first_map = lane < DIFF_DH

    def rot(x):
        return x * cos + pltpu.roll(x, HW - half, 1) * sa + pltpu.roll(x, half, 1) * sb

    for h in range(HEADS):
        hs = slice(h * HW, (h + 1) * HW)
        q = rot(q_ref[:, hs]) * (DIFF_DH ** -0.5)
        qm_ref[0, :, hs] = jnp.where(first_map, q, 0.0).astype(qm_ref.dtype)
        qm_ref[1, :, hs] = jnp.where(first_map, 0.0, q).astype(qm_ref.dtype)
        k = rot(k_ref[:, hs])
        kr_ref[:, h, :] = k
        kb_ref[:, hs] = k.astype(kb_ref.dtype)
        vf_ref[:, h, :] = v_ref[:, hs]
    vb_ref[...] = v_ref[...].astype(vb_ref.dtype)


def rope_split(z, tables, tr, act_dtype):
    t = z.shape[0]
    nl = tables[0].shape[0] // tr
    zspec = lambda off: pl.BlockSpec((tr, WIDTH), lambda i: (i, off // WIDTH))
    tspec = pl.BlockSpec((tr, HW), lambda i: (i % nl, 0))
    ospec = pl.BlockSpec((tr, WIDTH), lambda i: (i, 0))
    hspec = pl.BlockSpec((tr, HEADS, HW), lambda i: (i, 0, 0))
    return pl.pallas_call(
        _rope_body,
        grid=(t // tr,),
        in_specs=[zspec(_Z_DQ), zspec(_Z_DK), zspec(_Z_DV), tspec, tspec, tspec],
        out_specs=[pl.BlockSpec((2, tr, WIDTH), lambda i: (0, i, 0)), hspec, ospec, hspec, ospec],
        out_shape=[jax.ShapeDtypeStruct((2, t, WIDTH), act_dtype),
                   jax.ShapeDtypeStruct((t, HEADS, HW), F32),
                   jax.ShapeDtypeStruct((t, WIDTH), act_dtype),
                   jax.ShapeDtypeStruct((t, HEADS, HW), F32),
                   jax.ShapeDtypeStruct((t, WIDTH), act_dtype)],
        compiler_params=_params("parallel"),
        name="rope_split",
    )(z, z, z, *tables)


def _lambda(lq1_ref, lk1_ref, lq2_ref, lk2_ref, lam_init):
    s1 = jnp.sum(lq1_ref[...] * lk1_ref[...], axis=-1, keepdims=True)
    s2 = jnp.sum(lq2_ref[...] * lk2_ref[...], axis=-1, keepdims=True)
    return jnp.exp(s1) - jnp.exp(s2) + lam_init


_FLASH_ROWS = 256


def _flash_body(qi_ref, ki_ref, q_ref, k_ref, v_ref, lq1_ref, lk1_ref, lq2_ref, lk2_ref, sub_ref, o_ref,
                m_ref, acc_ref, *, lam_init):
    pair = pl.program_id(2)
    qi = qi_ref[pair]
    ki = ki_ref[pair]
    tq = q_ref.shape[1]
    tk = k_ref.shape[0]
    ratio = tq // tk
    rows = min(tq, _FLASH_ROWS)

    @pl.when(ki == 0)
    def _():
        m_ref[...] = jnp.full_like(m_ref, NEG)
        acc_ref[...] = jnp.zeros_like(acc_ref)

    def block(diag):
        k = k_ref[...]
        v = v_ref[...]
        v1 = jnp.concatenate([v, jnp.ones_like(v)], axis=1)
        col0 = 0 if diag is None else diag * tk
        chains = [(c, r0) for c in range(2) for r0 in range(0, tq, rows) if r0 + rows > col0]

        def n_keys(r0):
            return tk if diag is None else min(tk, r0 + rows - col0)

        def scores(c, r0):
            kc = n_keys(r0)
            s = _dot_nt(q_ref[c, r0:r0 + rows, :], k[0:kc])
            if diag is not None and r0 < col0 + kc - 1:
                ri = lax.broadcasted_iota(jnp.int32, s.shape, 0) + r0
                ci = lax.broadcasted_iota(jnp.int32, s.shape, 1) + col0
                s = jnp.where(ci <= ri, s, NEG)
            return s

        ahead = 2
        pending = [scores(*ch) for ch in chains[:ahead]]
        for i, (c, r0) in enumerate(chains):
            rs = slice(r0, r0 + rows)
            s = pending.pop(0)
            kc = n_keys(r0)
            tiles = [s[:, j * HW:(j + 1) * HW] for j in range(kc // HW)]
            tile_max = functools.reduce(jnp.maximum, tiles)
            m_prev = m_ref[c, rs, :]
            m_new = jnp.maximum(m_prev, jnp.max(tile_max, axis=-1, keepdims=True))
            alpha = jnp.exp(m_prev - m_new)
            p = jnp.concatenate([jnp.exp(tl - m_new).astype(BF16) for tl in tiles], axis=1)
            pv = _dot(p, v1[0:kc])
            if i + ahead < len(chains):
                pending.append(scores(*chains[i + ahead]))
            acc_ref[c, rs, :] = jnp.concatenate([alpha, alpha], axis=1) * acc_ref[c, rs, :] + pv
            m_ref[c, rs, :] = m_new

    pl.when(ki < qi * ratio)(functools.partial(block, None))
    for d in range(ratio):
        pl.when(ki == qi * ratio + d)(functools.partial(block, d))

    @pl.when(ki == qi * ratio + ratio - 1)
    def _():
        lam = _lambda(lq1_ref, lk1_ref, lq2_ref, lk2_ref, lam_init)
        o = (acc_ref[0, :, 0:HW] * (1.0 / acc_ref[0, :, HW:2 * HW])
             - lam * (acc_ref[1, :, 0:HW] * (1.0 / acc_ref[1, :, HW:2 * HW])))
        o_ref[...] = (_rms(o, sub_ref[...]) * (1.0 - lam_init)).astype(o_ref.dtype)


def flash_diff(qm, kb, vb, lparams, subln, lam_init, batch, tq, tk):
    t = kb.shape[0]
    l = t // batch
    nq = l // tq
    nk = l // tk
    assert tq % tk == 0
    pairs = [(i, j) for i in range(nq) for j in range((i + 1) * (tq // tk))]
    qi_arr = jnp.asarray([p[0] for p in pairs], jnp.int32)
    ki_arr = jnp.asarray([p[1] for p in pairs], jnp.int32)
    lspec = pl.BlockSpec((1, DIFF_DH), lambda b, h, p, qi, ki: (0, 0))
    kvspec = pl.BlockSpec((tk, HW), lambda b, h, p, qi, ki: (b * nk + ki[p], h))
    grid_spec = pltpu.PrefetchScalarGridSpec(
        num_scalar_prefetch=2,
        grid=(batch, HEADS, len(pairs)),
        in_specs=[pl.BlockSpec((2, tq, HW), lambda b, h, p, qi, ki: (0, b * nq + qi[p], h)),
                  kvspec, kvspec, lspec, lspec, lspec, lspec,
                  pl.BlockSpec((1, HW), lambda b, h, p, qi, ki: (0, 0))],
        out_specs=pl.BlockSpec((tq, HW), lambda b, h, p, qi, ki: (b * nq + qi[p], h)),
        scratch_shapes=[pltpu.VMEM((2, tq, HW), F32), pltpu.VMEM((2, tq, 2 * HW), F32)],
    )
    return pl.pallas_call(
        functools.partial(_flash_body, lam_init=lam_init),
        grid_spec=grid_spec,
        out_shape=jax.ShapeDtypeStruct((t, WIDTH), BF16),
        compiler_params=_params("parallel", "parallel", "arbitrary"),
        name="flash_diff",
    )(qi_arr, ki_arr, qm, kb, vb, *[p.reshape(1, DIFF_DH) for p in lparams], subln.reshape(1, HW))


_PAGES_PER_STEP = 16


def _paged_body(pt_ref, q_ref, *refs, lam_init, n_valid):
    del pt_ref
    pp = _PAGES_PER_STEP
    k_refs, v_refs = refs[:pp], refs[pp:2 * pp]
    (kn_ref, vn_ref, lq1_ref, lk1_ref, lq2_ref, lk2_ref, sub_ref, o_ref,
     m_ref, l_ref, acc_ref, kn_scr, vn_scr) = refs[2 * pp:]
    p_idx = pl.program_id(1)
    nq = q_ref.shape[1]

    @pl.when(p_idx == 0)
    def _():
        m_ref[...] = jnp.full_like(m_ref, NEG)
        l_ref[...] = jnp.zeros_like(l_ref)
        acc_ref[...] = jnp.zeros_like(acc_ref)

    def head_q(h):
        hs = slice(h * HW, (h + 1) * HW)
        return jnp.concatenate([q_ref[0, :, hs], q_ref[1, :, hs]], axis=0).astype(BF16)

    def update(k_page, v_page, n_pages, mask):
        qs = [head_q(h) for h in range(HEADS)]
        s = jnp.concatenate(
            [jnp.concatenate([_dot_nt(qs[h], k_page(j, h).astype(BF16)) for j in range(n_pages)], axis=-1)
             for h in range(HEADS)], axis=0)
        if mask is not None:
            s = jnp.where(mask, s, NEG)
        m_prev = m_ref[...]
        m_new = jnp.maximum(m_prev, jnp.max(s, axis=-1, keepdims=True))
        alpha = jnp.exp(m_prev - m_new)
        p = jnp.exp(s - m_new).astype(BF16)
        l_ref[...] = alpha * l_ref[...] + jnp.sum(p.astype(F32), axis=-1, keepdims=True)
        pvs = []
        for h in range(HEADS):
            ph = p[h * 2 * nq:(h + 1) * 2 * nq]
            pv = _dot(ph[:, 0:PAGE], v_page(0, h).astype(BF16))
            for j in range(1, n_pages):
                pv = pv + _dot(ph[:, j * PAGE:(j + 1) * PAGE], v_page(j, h).astype(BF16))
            pvs.append(pv)
        acc_ref[...] = alpha * acc_ref[...] + jnp.concatenate(pvs, axis=0)
        m_ref[...] = m_new

    head_rows = lambda h: pl.ds(h, PAGE, stride=HEADS)
    update(lambda j, h: k_refs[j][head_rows(h), :], lambda j, h: v_refs[j][head_rows(h), :], pp, None)

    @pl.when(p_idx == pl.num_programs(1) - 1)
    def _():
        kn_scr[...] = jnp.zeros_like(kn_scr)
        vn_scr[...] = jnp.zeros_like(vn_scr)
        kn_scr[0:nq, :] = kn_ref[...]
        vn_scr[0:nq, :] = vn_ref[...]
        ri = lax.broadcasted_iota(jnp.int32, (HEADS * 2 * nq, PAGE), 0) % nq
        ci = lax.broadcasted_iota(jnp.int32, (HEADS * 2 * nq, PAGE), 1)
        update(lambda j, h: kn_scr[:, h * HW:(h + 1) * HW], lambda j, h: vn_scr[:, h * HW:(h + 1) * HW], 1,
               (ci <= ri) & (ci < n_valid))
        lam = _lambda(lq1_ref, lk1_ref, lq2_ref, lk2_ref, lam_init)
        for h in range(HEADS):
            hs = slice(h * HW, (h + 1) * HW)
            r0 = slice(h * 2 * nq, h * 2 * nq + nq)
            r1 = slice(h * 2 * nq + nq, (h + 1) * 2 * nq)
            o = acc_ref[r0] * (1.0 / l_ref[r0]) - lam * (acc_ref[r1] * (1.0 / l_ref[r1]))
            o_ref[:, hs] = (_rms(o, sub_ref[...]) * (1.0 - lam_init)).astype(o_ref.dtype)


def paged_diff(qm, kb, vb, cache_k, cache_v, page_table, layer, lparams, subln, lam_init, n_valid):
    _, b, nq, _ = qm.shape
    n_pages = page_table.shape[1]
    pp = _PAGES_PER_STEP
    assert n_pages % pp == 0
    depth, n_pool = cache_k.shape[:2]
    cache_k = cache_k.reshape(depth, n_pool, PAGE * HEADS, HW)
    cache_v = cache_v.reshape(depth, n_pool, PAGE * HEADS, HW)
    page_spec = lambda j: pl.BlockSpec((None, None, PAGE * HEADS, HW),
                                       lambda i, p, pt: (layer, pt[i, p * pp + j], 0, 0))
    page_specs = [page_spec(j) for j in range(pp)]
    new_spec = pl.BlockSpec((None, nq, WIDTH), lambda i, p, pt: (i, 0, 0))
    lspec = pl.BlockSpec((1, DIFF_DH), lambda i, p, pt: (0, 0))
    grid_spec = pltpu.PrefetchScalarGridSpec(
        num_scalar_prefetch=1,
        grid=(b, n_pages // pp),
        in_specs=([pl.BlockSpec((2, None, nq, WIDTH), lambda i, p, pt: (0, i, 0, 0))]
                  + page_specs + page_specs
                  + [new_spec, new_spec, lspec, lspec, lspec, lspec,
                     pl.BlockSpec((1, HW), lambda i, p, pt: (0, 0))]),
        out_specs=pl.BlockSpec((None, nq, WIDTH), lambda i, p, pt: (i, 0, 0)),
        scratch_shapes=[pltpu.VMEM((HEADS * 2 * nq, 1), F32), pltpu.VMEM((HEADS * 2 * nq, 1), F32),
                        pltpu.VMEM((HEADS * 2 * nq, HW), F32),
                        pltpu.VMEM((PAGE, WIDTH), F32), pltpu.VMEM((PAGE, WIDTH), F32)],
    )
    return pl.pallas_call(
        functools.partial(_paged_body, lam_init=lam_init, n_valid=n_valid),
        grid_spec=grid_spec,
        out_shape=jax.ShapeDtypeStruct((b, nq, WIDTH), F32),
        compiler_params=_params("parallel", "arbitrary"),
        name="paged_diff",
    )(page_table, qm, *([cache_k] * pp), *([cache_v] * pp), kb, vb,
      *[p.reshape(1, DIFF_DH) for p in lparams], subln.reshape(1, HW))


def _gelu(x):
    return 0.5 * x * (1.0 + jnp.tanh(math.sqrt(2.0 / math.pi) * (x + 0.044715 * (x * x * x))))


def _gmlp_body(u_ref, v_ref, lg_ref, lb_ref, ws_ref, bst_ref, o_ref, gv_ref, v_scr, *, tr):
    u = _gelu(u_ref[...])
    v = _gelu(v_ref[...])
    mu = jnp.mean(v, axis=-1, keepdims=True)
    vc = v - mu
    v = vc * lax.rsqrt(jnp.mean(vc * vc, axis=-1, keepdims=True) + EPS) * lg_ref[...] + lb_ref[...]
    gv_ref[...] = v
    if tr < GMLP_CHUNK:
        v_scr[...] = jnp.zeros_like(v_scr)
        v_scr[0:tr, :] = v
    ii = lax.broadcasted_iota(jnp.int32, (tr, GMLP_CHUNK), 0)
    jj = lax.broadcasted_iota(jnp.int32, (tr, GMLP_CHUNK), 1)
    for g in range(HEADS):
        gs = slice(g * HW, (g + 1) * HW)
        w = jnp.where(ii >= jj, ws_ref[g, 0:tr, :], 0.0).astype(BF16)
        bias = bst_ref[0:tr, g:g + 1]
        for r0 in range(0, u.shape[0], tr):
            vg = (v_scr[:, gs] if tr < GMLP_CHUNK else v[r0:r0 + tr, gs]).astype(BF16)
            o_ref[r0:r0 + tr, gs] = (u[r0:r0 + tr, gs] * (_dot(w, vg) + bias)).astype(o_ref.dtype)


def gmlp(z, ln_g, ln_b, ws, bs, tr, act_dtype):
    t = z.shape[0]
    tm = _tile(t, 4 * tr) if tr == GMLP_CHUNK else tr
    zspec = lambda off: pl.BlockSpec((tm, WIDTH), lambda i: (i, off // WIDTH))
    ospec = pl.BlockSpec((tm, WIDTH), lambda i: (i, 0))
    vec = pl.BlockSpec((1, WIDTH), lambda i: (0, 0))
    return pl.pallas_call(
        functools.partial(_gmlp_body, tr=tr),
        grid=(t // tm,),
        in_specs=[zspec(_Z_CU), zspec(_Z_CV), vec, vec,
                  pl.BlockSpec((HEADS, GMLP_CHUNK, GMLP_CHUNK), lambda i: (0, 0, 0)),
                  pl.BlockSpec((GMLP_CHUNK, HEADS), lambda i: (0, 0))],
        out_specs=[ospec, ospec],
        out_shape=[jax.ShapeDtypeStruct((t, WIDTH), act_dtype), jax.ShapeDtypeStruct((t, WIDTH), F32)],
        scratch_shapes=[pltpu.VMEM((GMLP_CHUNK, WIDTH), F32)],
        compiler_params=_params("parallel"),
        name="gmlp",
    )(z, z, ln_g.reshape(1, WIDTH), ln_b.reshape(1, WIDTH), ws, bs.T)


def _merge_body(x_ref, gate_ref, oa_ref, ob_ref, oc_ref, wa_ref, wb_ref, wc_ref, wo_ref, o_ref):
    d = D_MODEL
    merged = (_sigmoid(gate_ref[:, 0:d]) * _dot(oa_ref[...].astype(BF16), wa_ref[...])
              + _sigmoid(gate_ref[:, d:2 * d]) * _dot(ob_ref[...].astype(BF16), wb_ref[...])
              + _sigmoid(gate_ref[:, 2 * d:3 * d]) * _dot(oc_ref[...].astype(BF16), wc_ref[...]))
    o_ref[...] = x_ref[...] + _dot(merged.astype(BF16), wo_ref[...])


def merge(x, z, oa, ob, oc, wa, wb, wc, wo, tm):
    t, d = x.shape
    row = lambda w: pl.BlockSpec((tm, w), lambda i: (i, 0))
    full = lambda a: pl.BlockSpec(a.shape, lambda i: (0, 0))
    return pl.pallas_call(
        _merge_body,
        grid=(t // tm,),
        in_specs=[row(d), row(3 * d), row(WIDTH), row(WIDTH), row(WIDTH),
                  full(wa), full(wb), full(wc), full(wo)],
        out_specs=row(d),
        out_shape=jax.ShapeDtypeStruct((t, d), F32),
        compiler_params=_params("parallel"),
        name="merge",
    )(x, z, oa, ob, oc, wa, wb, wc, wo)


def _xattn_body(x_ref, g_ref, wq_ref, mk_ref, mv_ref, wo_ref, o_ref, *, interleaved):
    x = x_ref[...]
    q = _dot(_rms(x, g_ref[...]).astype(BF16), wq_ref[...])
    if interleaved:
        m = mk_ref.shape[0] // HEADS
        head = lambda ref, h: ref[pl.ds(h, m, stride=HEADS), :]
    else:
        head = lambda ref, h: ref[:, h * HW:(h + 1) * HW]
    ss = [_dot_nt(q[:, h * HW:(h + 1) * HW].astype(BF16), head(mk_ref, h).astype(BF16)) * (HW ** -0.5)
          for h in range(HEADS)]
    es = [jnp.exp(s - jnp.max(s, axis=-1, keepdims=True)) for s in ss]
    ps = [(e * (1.0 / jnp.sum(e, axis=-1, keepdims=True))).astype(BF16) for e in es]
    o = jnp.concatenate([_dot(p, head(mv_ref, h).astype(BF16)) for h, p in enumerate(ps)], axis=-1)
    o_ref[...] = x + _dot(o.astype(BF16), wo_ref[...])


def xattn(x, g, wq, mem_k, mem_v, layer, wo, tm, tiles_per_seq):
    t, d = x.shape
    interleaved = mem_k.ndim == 5
    if interleaved:
        mem_k = mem_k.reshape(mem_k.shape[:2] + (mem_k.shape[2] * HEADS, HW))
        mem_v = mem_v.reshape(mem_k.shape)
    mem_spec = pl.BlockSpec((None, None) + mem_k.shape[2:], lambda i: (layer, i // tiles_per_seq, 0, 0))
    return pl.pallas_call(
        functools.partial(_xattn_body, interleaved=interleaved),
        grid=(t // tm,),
        in_specs=[pl.BlockSpec((tm, d), lambda i: (i, 0)),
                  pl.BlockSpec((1, d), lambda i: (0, 0)),
                  pl.BlockSpec(wq.shape, lambda i: (0, 0)),
                  mem_spec, mem_spec,
                  pl.BlockSpec(wo.shape, lambda i: (0, 0))],
        out_specs=pl.BlockSpec((tm, d), lambda i: (i, 0)),
        out_shape=jax.ShapeDtypeStruct((t, d), F32),
        compiler_params=_params("parallel"),
        name="xattn",
    )(x, g.reshape(1, d), wq, mem_k, mem_v, wo)


def _ffn_body(x_ref, g_ref, w1_ref, w2_ref, o_ref, h_ref):
    @pl.when(pl.program_id(1) == 0)
    def _():
        x = x_ref[...]
        h_ref[...] = _rms(x, g_ref[...]).astype(BF16)
        o_ref[...] = x

    a = jnp.maximum(_dot(h_ref[...], w1_ref[...]), 0.0)
    o_ref[...] += _dot((a * a).astype(BF16), w2_ref[...])


def ffn(x, g, w1, w2, tm, tf):
    t, d = x.shape
    f = w1.shape[1]
    return pl.pallas_call(
        _ffn_body,
        grid=(t // tm, f // tf),
        in_specs=[pl.BlockSpec((tm, d), lambda i, j: (i, 0)),
                  pl.BlockSpec((1, d), lambda i, j: (0, 0)),
                  pl.BlockSpec((d, tf), lambda i, j: (0, j)),
                  pl.BlockSpec((tf, d), lambda i, j: (j, 0))],
        out_specs=pl.BlockSpec((tm, d), lambda i, j: (i, 0)),
        out_shape=jax.ShapeDtypeStruct((t, d), F32),
        scratch_shapes=[pltpu.VMEM((tm, d), BF16)],
        compiler_params=_params("parallel", "arbitrary"),
        name="ffn",
    )(x, g.reshape(1, d), w1, w2)


def _final_norm_body(x_ref, g_ref, o_ref):
    o_ref[...] = _rms(x_ref[...], g_ref[...])


def final_norm(x, g, tm):
    t, d = x.shape
    return pl.pallas_call(
        _final_norm_body,
        grid=(t // tm,),
        in_specs=[pl.BlockSpec((tm, d), lambda i: (i, 0)), pl.BlockSpec((1, d), lambda i: (0, 0))],
        out_specs=pl.BlockSpec((tm, d), lambda i: (i, 0)),
        out_shape=jax.ShapeDtypeStruct((t, d), F32),
        compiler_params=_params("parallel"),
        name="final_norm",
    )(x, g.reshape(1, d))


def _regroup_w_in(w):
    o = 0
    parts = {}
    for name, size in (("q", WIDTH), ("k", WIDTH), ("v", WIDTH), ("a", HEADS), ("b", HEADS), ("g", WIDTH),
                       ("dq", WIDTH), ("dk", WIDTH), ("dv", WIDTH), ("cu", WIDTH), ("cv", WIDTH),
                       ("gates", 3 * D_MODEL)):
        parts[name] = w[:, o:o + size]
        o += size
    used = _Z_AB + 2 * HEADS
    cols = [parts[n] for n in ("gates", "q", "k", "v", "g", "dq", "dk", "dv", "cu", "cv", "a", "b")]
    cols.append(jnp.zeros((w.shape[0], _Z_COLS - used), w.dtype))
    return jnp.concatenate(cols, axis=1).astype(BF16)


def _tile(n, want):
    while n % want:
        want //= 2
    return want


def kernel(x_prompt, x_sample, cache_diff_k, cache_diff_v, page_table, cache_mem_k, cache_mem_v, state_gdn, state_gdn_conv, mem_prompt, w_in, norm_mix, gdn_conv_w, gdn_A_log, gdn_dt_bias, gdn_o_norm, diff_lq1, diff_lk1, diff_lq2, diff_lk2, diff_subln, gmlp_ln_g, gmlp_ln_b, gmlp_ws, gmlp_b, w_br_gdn, w_br_diff, w_br_gmlp, w_out, norm_xa, norm_mem, xa_wq, xa_wk, xa_wv, xa_wo, norm_ffn, ffn_w1, ffn_w2, norm_final):
    depth = w_in.shape[0]
    bp, lp, d = x_prompt.shape
    bs, ls, _ = x_sample.shape
    mem_len = mem_prompt.shape[1]
    past_len = page_table.shape[1] * PAGE
    lpad = SAMPLE_PAD
    bf = lambda a: a.astype(BF16)

    mem_tok = mem_prompt.reshape(bp * mem_len, d)
    mem_kv = [norm_matmul(mem_tok, norm_mem[l], bf(jnp.concatenate([xa_wk[l], xa_wv[l]], axis=1)),
                          _tile(bp * mem_len, 512), 512) for l in range(depth)]
    mem_k_p = jnp.stack([kv[:, :WIDTH] for kv in mem_kv]).reshape(depth, bp, mem_len, WIDTH)
    mem_v_p = jnp.stack([kv[:, WIDTH:] for kv in mem_kv]).reshape(depth, bp, mem_len, WIDTH)

    weights = [dict(w_in=_regroup_w_in(w_in[l]), wa=bf(w_br_gdn[l]), wb=bf(w_br_diff[l]), wc=bf(w_br_gmlp[l]),
                    wo=bf(w_out[l]), xq=bf(xa_wq[l]), xo=bf(xa_wo[l]), w1=bf(ffn_w1[l]), w2=bf(ffn_w2[l]))
               for l in range(depth)]

    def trunk(x, b, l, l_valid, pos, mem_k, mem_v, s_in, buf_in, paged):
        t = b * l
        prompt = paged is None
        act = BF16 if prompt else F32
        tm = _tile(t, 1024)
        tables = _rope_tables(pos)
        outs = dict(k=[], v=[], s=[], buf=[], gv=[])
        for li in range(depth):
            w = weights[li]
            lam_init = 0.8 - 0.6 * math.exp(-0.3 * li)
            lparams = (diff_lq1[li], diff_lk1[li], diff_lq2[li], diff_lk2[li])
            z = norm_matmul(x, norm_mix[li], w["w_in"], tm, 1024)
            z3 = z.reshape(b, l, _Z_COLS)
            o_a, s_new = gdn(z3, buf_in[li], s_in, li, gdn_conv_w[li], gdn_A_log[li], gdn_dt_bias[li],
                             gdn_o_norm[li], _tile(l, 256) if prompt else GDN_CHUNK, _tile(l, 256) if prompt else l,
                             l_valid, act)
            outs["buf"].append(z3[:, l_valid - (GDN_CONV - 1):l_valid, _Z_QKV:_Z_QKV + GDN_CONV_CH])
            qm, kr, kb, vf, vb = rope_split(z, tables, _tile(l, 512), act)
            if prompt:
                o_b = flash_diff(qm, kb, vb, lparams, diff_subln[li], lam_init, b, _tile(l, 1024), _tile(l, 1024))
            else:
                ck, cv, pt = paged
                o_b = paged_diff(qm.reshape(2, b, l, WIDTH), kb.reshape(b, l, WIDTH), vb.reshape(b, l, WIDTH),
                                 ck, cv, pt, li, lparams, diff_subln[li], lam_init, l_valid)
                o_b = o_b.reshape(t, WIDTH)
            o_c, gv = gmlp(z, gmlp_ln_g[li], gmlp_ln_b[li], gmlp_ws[li], gmlp_b[li], min(l, GMLP_CHUNK), act)
            x = merge(x, z, o_a.reshape(t, WIDTH), o_b, o_c, w["wa"], w["wb"], w["wc"], w["wo"], _tile(t, 512))
            xa_tm = _tile(l, 512)
            x = xattn(x, norm_xa[li], w["xq"], mem_k, mem_v, li, w["xo"], xa_tm, l // xa_tm)
            x = ffn(x, norm_ffn[li], w["w1"], w["w2"], tm, 1024)
            outs["k"].append(kr)
            outs["v"].append(vf)
            outs["s"].append(s_new)
            outs["gv"].append(gv)
        y = final_norm(x, norm_final, tm)
        seq = lambda a: a.reshape((depth, b, l) + a.shape[2:])[:, :, :l_valid]
        return (y.reshape(b, l, d)[:, :l_valid], seq(jnp.stack(outs["k"])), seq(jnp.stack(outs["v"])),
                jnp.stack(outs["s"]), jnp.stack(outs["buf"]), seq(jnp.stack(outs["gv"])))

    s0_p = jnp.zeros((depth, bp, HEADS, HW, HW), F32)
    buf0_p = jnp.zeros((depth, bp, GDN_CONV - 1, GDN_CONV_CH), F32)
    (y_p, k_p, v_p, s_p, buf_p, _) = trunk(x_prompt.reshape(bp * lp, d), bp, lp, lp, jnp.arange(lp),
                                           mem_k_p, mem_v_p, s0_p, buf0_p, None)

    x_s = jnp.pad(x_sample, ((0, 0), (0, lpad - ls), (0, 0))).reshape(bs * lpad, d)
    (y_s, k_s, v_s, s_s, buf_s, gv_s) = trunk(x_s, bs, lpad, ls, past_len + jnp.arange(lpad),
                                              cache_mem_k, cache_mem_v, state_gdn, state_gdn_conv,
                                              (cache_diff_k, cache_diff_v, page_table))

    return (y_p, y_s, k_p, v_p,
            mem_k_p.reshape(depth, bp, mem_len, HEADS, HW), mem_v_p.reshape(depth, bp, mem_len, HEADS, HW),
            s_p, buf_p, k_s, v_s, s_s, buf_s, gv_s)
```

```python
import functools
import math

import jax
import jax.numpy as jnp
from jax import lax
from jax.experimental import pallas as pl
from jax.experimental.pallas import tpu as pltpu

F32 = jnp.float32
BF16 = jnp.bfloat16
EPS = 1e-6
NEG = -1e30

D_MODEL = 1024
HEADS = 4
HW = 128
WIDTH = HEADS * HW
DIFF_DH = 64
ROPE_DIM = 16
ROPE_THETA = 500000.0
GDN_CONV = 4
GDN_CHUNK = 64
GDN_CONV_CH = 3 * WIDTH
GMLP_CHUNK = 128
PAGE = 128
SAMPLE_PAD = 8

_ZA_GATES = 0
_ZA_GIN = 3 * D_MODEL
_ZA_CU = _ZA_GIN + WIDTH
_ZB_QKV = 0
_ZB_DQ = GDN_CONV_CH
_ZB_DK = _ZB_DQ + WIDTH
_ZB_DV = _ZB_DK + WIDTH
_ZB_CV = _ZB_DV + WIDTH
_ZB_AB = _ZB_CV + WIDTH
_Z_HALF = 4096

_VMEM_LIMIT = 56 * 1024 * 1024


def _params(*sem):
    return pltpu.CompilerParams(dimension_semantics=sem, vmem_limit_bytes=_VMEM_LIMIT)


def _rms(x, g):
    return x * lax.rsqrt(jnp.mean(x * x, axis=-1, keepdims=True) + EPS) * g


def _sigmoid(x):
    return 1.0 / (1.0 + jnp.exp(-x))


def _dot(a, b):
    return jnp.dot(a, b, preferred_element_type=F32)


def _dot_nt(a, b):
    return lax.dot_general(a, b, (((1,), (1,)), ((), ())), preferred_element_type=F32)


def _dot_tn(a, b):
    return lax.dot_general(a, b, (((0,), (0,)), ((), ())), preferred_element_type=F32)


def _split(a):
    hi = lax.bitcast_convert_type(lax.bitcast_convert_type(a, jnp.int32) & jnp.int32(-65536), F32)
    return hi, a - hi


def _hi_lhs(a):
    hi, lo = _split(a)
    return jnp.concatenate([hi, hi, lo], axis=1).astype(BF16)


def _hi_rhs(b):
    hi, lo = _split(b)
    return jnp.concatenate([hi, lo, hi], axis=0).astype(BF16)


def _dot_hi(a, b):
    return _dot(_hi_lhs(a), _hi_rhs(b))


def _dot_hi_exact_lhs(a_bf16, b):
    hi, lo = _split(b)
    lo_hi, lo_lo = _split(lo)
    return _dot(jnp.concatenate([a_bf16] * 3, axis=1), jnp.concatenate([hi, lo_hi, lo_lo], axis=0).astype(BF16))


def _dot_hi_exact_rhs(a, b_bf16):
    hi, lo = _split(a)
    lo_hi, lo_lo = _split(lo)
    return _dot(jnp.concatenate([hi, lo_hi, lo_lo], axis=1).astype(BF16), jnp.concatenate([b_bf16] * 3, axis=0))


def _norm_mm_body(x_ref, g_ref, w_ref, o_ref, h_ref):
    @pl.when(pl.program_id(1) == 0)
    def _():
        h_ref[...] = _rms(x_ref[...], g_ref[...]).astype(BF16)

    o_ref[...] = _dot(h_ref[...], w_ref[...]).astype(o_ref.dtype)


def norm_matmul(x, g, w, tm, tn):
    t, d = x.shape
    n = w.shape[1]
    return pl.pallas_call(
        _norm_mm_body,
        grid=(t // tm, n // tn),
        in_specs=[pl.BlockSpec((tm, d), lambda i, j: (i, 0)),
                  pl.BlockSpec((1, d), lambda i, j: (0, 0)),
                  pl.BlockSpec((d, tn), lambda i, j: (0, j))],
        out_specs=pl.BlockSpec((tm, tn), lambda i, j: (i, j)),
        out_shape=jax.ShapeDtypeStruct((t, n), F32),
        scratch_shapes=[pltpu.VMEM((tm, d), BF16)],
        compiler_params=_params("parallel", "arbitrary"),
        name="norm_matmul",
    )(x, g.reshape(1, d), w)


def _proj_in_body(x_ref, g_ref, w_ref, oa_ref, ob_ref, h_ref, *, na):
    j = pl.program_id(1)

    @pl.when(j == 0)
    def _():
        h_ref[...] = _rms(x_ref[...], g_ref[...]).astype(BF16)

    @pl.when(j < na)
    def _():
        oa_ref[...] = _dot(h_ref[...], w_ref[...]).astype(oa_ref.dtype)

    @pl.when(j >= na)
    def _():
        ob_ref[...] = _dot(h_ref[...], w_ref[...])


def proj_in(x, g, w, tm, tn, act_dtype):
    t, d = x.shape
    na = _Z_HALF // tn
    return pl.pallas_call(
        functools.partial(_proj_in_body, na=na),
        grid=(t // tm, 2 * na),
        in_specs=[pl.BlockSpec((tm, d), lambda i, j: (i, 0)),
                  pl.BlockSpec((1, d), lambda i, j: (0, 0)),
                  pl.BlockSpec((d, tn), lambda i, j: (0, j))],
        out_specs=[pl.BlockSpec((tm, tn), lambda i, j: (i, jnp.minimum(j, na - 1))),
                   pl.BlockSpec((tm, tn), lambda i, j: (i, jnp.maximum(j - na, 0)))],
        out_shape=[jax.ShapeDtypeStruct((t, _Z_HALF), act_dtype), jax.ShapeDtypeStruct((t, _Z_HALF), F32)],
        scratch_shapes=[pltpu.VMEM((tm, d), BF16)],
        compiler_params=_params("parallel", "arbitrary"),
        name="proj_in",
    )(x, g.reshape(1, d), w)


def _gdn_body(qkv_ref, gin_ref, ab_ref, buf_ref, s0_ref, cw_ref, alog_ref, dtb_ref, onorm_ref,
              o_ref, sout_ref, xs_ref, ab_scr, s_ref, *, tl, tl_in, valid):
    c = GDN_CHUNK
    nc = tl // c
    t = pl.program_id(1)

    @pl.when(t == 0)
    def _():
        s_ref[...] = s0_ref[...]
        xs_ref[...] = jnp.zeros_like(xs_ref)
        xs_ref[5:8, :] = buf_ref[...]
        ab_scr[...] = jnp.zeros_like(ab_scr)

    xs_ref[8:8 + tl_in, :] = qkv_ref[...]
    ab_scr[0:tl_in, :] = ab_ref[...]
    w = cw_ref[...]
    y = (xs_ref[5:5 + tl, :] * w[0:1] + xs_ref[6:6 + tl, :] * w[1:2]
         + xs_ref[7:7 + tl, :] * w[2:3] + xs_ref[8:8 + tl, :] * w[3:4])
    xs_ref[5:8, :] = xs_ref[tl + 5:tl + 8, :]
    y = y * _sigmoid(y)

    ab = ab_scr[...]
    sp_in = ab + dtb_ref[...]
    softplus = jnp.maximum(sp_in, 0.0) + jnp.log1p(jnp.exp(-jnp.abs(sp_in)))
    g = -jnp.exp(alog_ref[...]) * softplus
    beta = _sigmoid(ab)
    if valid < tl:
        live = lax.broadcasted_iota(jnp.int32, (tl, 1), 0) < valid
        g = jnp.where(live, g, 0.0)
        beta = jnp.where(live, beta, 0.0)

    ti = lax.broadcasted_iota(jnp.int32, (tl, tl), 0)
    tj = lax.broadcasted_iota(jnp.int32, (tl, tl), 1)
    same_chunk = (ti // c) == (tj // c)
    gcol = _dot_hi_exact_lhs(jnp.where(same_chunk & (ti >= tj), 1.0, 0.0).astype(BF16), g)
    grow = _dot_hi_exact_rhs(g.T[0:8, :], jnp.where(same_chunk & (ti <= tj), 1.0, 0.0).astype(BF16))

    ii = lax.broadcasted_iota(jnp.int32, (c, c), 0)
    jj = lax.broadcasted_iota(jnp.int32, (c, c), 1)
    incl = ii >= jj
    strict = ii > jj
    eye = (ii == jj).astype(F32)

    def prepare(ci, h):
        rs = slice(ci * c, (ci + 1) * c)
        gi = gcol[rs, h:h + 1]
        gj = grow[h:h + 1, rs]
        dec_incl = jnp.exp(jnp.where(incl, gi - gj, NEG))
        glast = gcol[(ci + 1) * c - 1:(ci + 1) * c, h:h + 1]
        qh = y[rs, h * HW:(h + 1) * HW]
        kh = y[rs, WIDTH + h * HW:WIDTH + (h + 1) * HW]
        qh = qh * lax.rsqrt(jnp.sum(qh * qh, axis=-1, keepdims=True) + EPS) * (HW ** -0.5)
        kh = kh * lax.rsqrt(jnp.sum(kh * kh, axis=-1, keepdims=True) + EPS)
        return dict(dec_incl=dec_incl, dec_strict=jnp.where(strict, dec_incl, 0.0), gexp=jnp.exp(gi),
                    bcol=beta[rs, 4 + h:5 + h], kh=kh, vh=y[rs, 2 * WIDTH + h * HW:2 * WIDTH + (h + 1) * HW],
                    qb=qh.astype(BF16), kb=kh.astype(BF16), tail=jnp.exp(glast - gi), gend=jnp.exp(glast))

    terms = [prepare(ci, h) for ci in range(nc) for h in range(HEADS)]
    for m in terms:
        kk = _dot_nt(m["kb"], m["kb"])
        m["p"] = (_dot_nt(m["qb"], m["kb"]) * m["dec_incl"]).astype(BF16)
        m["x"] = -(m["bcol"] * kk * m["dec_strict"])
        m["tinv"] = eye + m["x"]
    for m in terms:
        m["x"] = _dot_hi(m["x"], m["x"])
    for _ in range(int(math.log2(c)) - 2):
        for m in terms:
            r = _dot_hi(jnp.concatenate([m["x"], m["tinv"]], axis=0), m["x"])
            m["x"] = r[0:c]
            m["tinv"] = m["tinv"] + r[c:2 * c]
    for m in terms:
        m["tinv"] = m["tinv"] + _dot_hi(m["tinv"], m["x"])
    for m in terms:
        sol = _dot_hi(m["tinv"], jnp.concatenate([m["bcol"] * m["vh"], (m["bcol"] * m["gexp"]) * m["kh"]], axis=-1))
        m["u0"] = sol[:, :HW]
        m["w"] = sol[:, HW:].astype(BF16)

    state = [s_ref[h] for h in range(HEADS)]
    for ci in range(nc):
        ms = terms[ci * HEADS:(ci + 1) * HEADS]
        sbs = [s.astype(BF16) for s in state]
        us = [m["u0"] - _dot_nt(m["w"], sb) for m, sb in zip(ms, sbs)]
        qs = [_dot_nt(m["qb"], sb) for m, sb in zip(ms, sbs)]
        pus = [_dot(m["p"], u.astype(BF16)) for m, u in zip(ms, us)]
        state = [m["gend"] * s + _dot_tn((u * m["tail"]).astype(BF16), m["kb"])
                 for m, s, u in zip(ms, state, us)]
        r0 = ci * c
        rows = min(c, tl_in - r0)
        if rows > 0:
            for h, (m, qs_h, pu) in enumerate(zip(ms, qs, pus)):
                hs = slice(h * HW, (h + 1) * HW)
                o = _rms(m["gexp"] * qs_h + pu, onorm_ref[...])[0:rows]
                gin = gin_ref[r0:r0 + rows, hs].astype(F32)
                o_ref[r0:r0 + rows, hs] = (o * (gin * _sigmoid(gin))).astype(o_ref.dtype)
    for h in range(HEADS):
        s_ref[h] = state[h]

    @pl.when(t == pl.num_programs(1) - 1)
    def _():
        sout_ref[...] = s_ref[...]


def gdn(za3, zb3, conv_buf, s0, layer, conv_w, a_log, dt_bias, o_norm, tl, tl_in, valid, act_dtype):
    b, l, _ = zb3.shape
    nt = l // tl_in
    assert tl % GDN_CHUNK == 0 and (tl_in == tl or nt == 1)
    lane_pad = lambda v, off: jnp.zeros((1, HW), F32).at[0, off:off + HEADS].set(v)
    body = functools.partial(_gdn_body, tl=tl, tl_in=tl_in, valid=valid)
    return pl.pallas_call(
        body,
        grid=(b, nt),
        in_specs=[pl.BlockSpec((None, tl_in, GDN_CONV_CH), lambda i, t: (i, t, _ZB_QKV // GDN_CONV_CH)),
                  pl.BlockSpec((None, tl_in, WIDTH), lambda i, t: (i, t, _ZA_GIN // WIDTH)),
                  pl.BlockSpec((None, tl_in, HW), lambda i, t: (i, t, _ZB_AB // HW)),
                  pl.BlockSpec((None, GDN_CONV - 1, GDN_CONV_CH), lambda i, t: (i, 0, 0)),
                  pl.BlockSpec((None, None, HEADS, HW, HW), lambda i, t: (layer, i, 0, 0, 0)),
                  pl.BlockSpec((GDN_CONV, GDN_CONV_CH), lambda i, t: (0, 0)),
                  pl.BlockSpec((1, HW), lambda i, t: (0, 0)),
                  pl.BlockSpec((1, HW), lambda i, t: (0, 0)),
                  pl.BlockSpec((1, HW), lambda i, t: (0, 0))],
        out_specs=[pl.BlockSpec((None, tl_in, WIDTH), lambda i, t: (i, t, 0)),
                   pl.BlockSpec((None, HEADS, HW, HW), lambda i, t: (i, 0, 0, 0))],
        out_shape=[jax.ShapeDtypeStruct((b, l, WIDTH), act_dtype),
                   jax.ShapeDtypeStruct((b, HEADS, HW, HW), F32)],
        scratch_shapes=[pltpu.VMEM((tl + 8, GDN_CONV_CH), F32),
                        pltpu.VMEM((tl, HW), F32),
                        pltpu.VMEM((HEADS, HW, HW), F32)],
        compiler_params=_params("parallel", "arbitrary"),
        name="gdn",
    )(zb3, za3, zb3, conv_buf, s0, conv_w, lane_pad(a_log, 0), lane_pad(dt_bias, 0), o_norm.reshape(1, HW))


def _rope_tables(pos):
    half = ROPE_DIM // 2
    inv = ROPE_THETA ** (-jnp.arange(0, ROPE_DIM, 2, dtype=F32) / ROPE_DIM)
    ang = pos.astype(F32)[:, None] * inv[None, :]
    m = jnp.arange(HW) % DIFF_DH
    cos = jnp.cos(ang)[:, m % half]
    sin = jnp.sin(ang)[:, m % half]
    cos_t = jnp.where(m < ROPE_DIM, cos, 1.0)
    sin_a = jnp.where(m < half, -sin, 0.0)
    sin_b = jnp.where((m >= half) & (m < ROPE_DIM), sin, 0.0)
    return cos_t.astype(F32), sin_a.astype(F32), sin_b.astype(F32)


def _rope_body(q_ref, k_ref, v_ref, cos_ref, sa_ref, sb_ref, *refs):
    qm_ref, kr_ref, kb_ref, vf_ref, vb_ref = refs[-5:]
    half = ROPE_DIM // 2
    cos, sa, sb = cos_ref[...], sa_ref[...], sb_ref[...]
    lane = lax.broadcasted_iota(jnp.int32, (1, HW), 1)
    first_map = lane < DIFF_DH

    def rot(x):
        return x * cos + pltpu.roll(x, HW - half, 1) * sa + pltpu.roll(x, half, 1) * sb

    for h in range(HEADS):
        hs = slice(h * HW, (h + 1) * HW)
        q = rot(q_ref[:, hs]) * (DIFF_DH ** -0.5)
        qm_ref[0, :, hs] = jnp.where(first_map, q, 0.0).astype(qm_ref.dtype)
        qm_ref[1, :, hs] = jnp.where(first_map, 0.0, q).astype(qm_ref.dtype)
        k = rot(k_ref[:, hs])
        kr_ref[:, h, :] = k
        kb_ref[:, hs] = k.astype(kb_ref.dtype)
        vf_ref[:, h, :] = v_ref[:, hs]
    vb_ref[...] = v_ref[...].astype(vb_ref.dtype)


def rope_split(z, tables, tr, act_dtype, layer, depth, k_all, v_all):
    t = z.shape[0]
    nl = tables[0].shape[0] // tr
    zspec = lambda off: pl.BlockSpec((tr, WIDTH), lambda i: (i, off // WIDTH))
    tspec = pl.BlockSpec((tr, HW), lambda i: (i % nl, 0))
    ospec = pl.BlockSpec((tr, WIDTH), lambda i: (i, 0))
    hspec = pl.BlockSpec((None, tr, HEADS, HW), lambda i: (layer, i, 0, 0))
    carried = [] if k_all is None else [k_all, v_all]
    return pl.pallas_call(
        _rope_body,
        grid=(t // tr,),
        in_specs=[zspec(_ZB_DQ), zspec(_ZB_DK), zspec(_ZB_DV), tspec, tspec, tspec]
                 + [pl.BlockSpec(memory_space=pl.ANY)] * len(carried),
        out_specs=[pl.BlockSpec((2, tr, WIDTH), lambda i: (0, i, 0)), hspec, ospec, hspec, ospec],
        out_shape=[jax.ShapeDtypeStruct((2, t, WIDTH), act_dtype),
                   jax.ShapeDtypeStruct((depth, t, HEADS, HW), F32),
                   jax.ShapeDtypeStruct((t, WIDTH), act_dtype),
                   jax.ShapeDtypeStruct((depth, t, HEADS, HW), F32),
                   jax.ShapeDtypeStruct((t, WIDTH), act_dtype)],
        input_output_aliases={6: 1, 7: 3} if carried else {},
        compiler_params=_params("parallel"),
        name="rope_split",
    )(z, z, z, *tables, *carried)


def _lambda(lq1_ref, lk1_ref, lq2_ref, lk2_ref, lam_init):
    s1 = jnp.sum(lq1_ref[...] * lk1_ref[...], axis=-1, keepdims=True)
    s2 = jnp.sum(lq2_ref[...] * lk2_ref[...], axis=-1, keepdims=True)
    return jnp.exp(s1) - jnp.exp(s2) + lam_init


_FLASH_ROWS = 256


def _flash_body(qi_ref, ki_ref, q_ref, k_ref, v_ref, lq1_ref, lk1_ref, lq2_ref, lk2_ref, sub_ref, o_ref,
                m_ref, acc_ref, *, lam_init):
    pair = pl.program_id(2)
    qi = qi_ref[pair]
    ki = ki_ref[pair]
    tq = q_ref.shape[1]
    tk = k_ref.shape[0]
    ratio = tq // tk
    rows = min(tq, _FLASH_ROWS)

    @pl.when(ki == 0)
    def _():
        m_ref[...] = jnp.full_like(m_ref, NEG)
        acc_ref[...] = jnp.zeros_like(acc_ref)

    def block(diag):
        k = k_ref[...]
        v = v_ref[...]
        v1 = jnp.concatenate([v, jnp.ones_like(v)], axis=1)
        col0 = 0 if diag is None else diag * tk
        chains = [(c, r0) for c in range(2) for r0 in range(0, tq, rows) if r0 + rows > col0]

        def n_keys(r0):
            return tk if diag is None else min(tk, r0 + rows - col0)

        def scores(c, r0):
            kc = n_keys(r0)
            s = _dot_nt(q_ref[c, r0:r0 + rows, :], k[0:kc])
            if diag is not None and r0 < col0 + kc - 1:
                ri = lax.broadcasted_iota(jnp.int32, s.shape, 0) + r0
                ci = lax.broadcasted_iota(jnp.int32, s.shape, 1) + col0
                s = jnp.where(ci <= ri, s, NEG)
            return s

        ahead = 2
        pending = [scores(*ch) for ch in chains[:ahead]]
        for i, (c, r0) in enumerate(chains):
            rs = slice(r0, r0 + rows)
            s = pending.pop(0)
            kc = n_keys(r0)
            tiles = [s[:, j * HW:(j + 1) * HW] for j in range(kc // HW)]
            tile_max = functools.reduce(jnp.maximum, tiles)
            m_prev = m_ref[c, rs, :]
            m_new = jnp.maximum(m_prev, jnp.max(tile_max, axis=-1, keepdims=True))
            alpha = jnp.exp(m_prev - m_new)
            p = jnp.concatenate([jnp.exp(tl - m_new).astype(BF16) for tl in tiles], axis=1)
            pv = _dot(p, v1[0:kc])
            if i + ahead < len(chains):
                pending.append(scores(*chains[i + ahead]))
            acc_ref[c, rs, :] = jnp.concatenate([alpha, alpha], axis=1) * acc_ref[c, rs, :] + pv
            m_ref[c, rs, :] = m_new

    pl.when(ki < qi * ratio)(functools.partial(block, None))
    for d in range(ratio):
        pl.when(ki == qi * ratio + d)(functools.partial(block, d))

    @pl.when(ki == qi * ratio + ratio - 1)
    def _():
        lam = _lambda(lq1_ref, lk1_ref, lq2_ref, lk2_ref, lam_init)
        o = (acc_ref[0, :, 0:HW] * (1.0 / acc_ref[0, :, HW:2 * HW])
             - lam * (acc_ref[1, :, 0:HW] * (1.0 / acc_ref[1, :, HW:2 * HW])))
        o_ref[...] = (_rms(o, sub_ref[...]) * (1.0 - lam_init)).astype(o_ref.dtype)


def flash_diff(qm, kb, vb, lparams, subln, lam_init, batch, tq, tk):
    t = kb.shape[0]
    l = t // batch
    nq = l // tq
    nk = l // tk
    assert tq % tk == 0
    pairs = [(i, j) for i in range(nq) for j in range((i + 1) * (tq // tk))]
    qi_arr = jnp.asarray([p[0] for p in pairs], jnp.int32)
    ki_arr = jnp.asarray([p[1] for p in pairs], jnp.int32)
    lspec = pl.BlockSpec((1, DIFF_DH), lambda b, h, p, qi, ki: (0, 0))
    kvspec = pl.BlockSpec((tk, HW), lambda b, h, p, qi, ki: (b * nk + ki[p], h))
    grid_spec = pltpu.PrefetchScalarGridSpec(
        num_scalar_prefetch=2,
        grid=(batch, HEADS, len(pairs)),
        in_specs=[pl.BlockSpec((2, tq, HW), lambda b, h, p, qi, ki: (0, b * nq + qi[p], h)),
                  kvspec, kvspec, lspec, lspec, lspec, lspec,
                  pl.BlockSpec((1, HW), lambda b, h, p, qi, ki: (0, 0))],
        out_specs=pl.BlockSpec((tq, HW), lambda b, h, p, qi, ki: (b * nq + qi[p], h)),
        scratch_shapes=[pltpu.VMEM((2, tq, HW), F32), pltpu.VMEM((2, tq, 2 * HW), F32)],
    )
    return pl.pallas_call(
        functools.partial(_flash_body, lam_init=lam_init),
        grid_spec=grid_spec,
        out_shape=jax.ShapeDtypeStruct((t, WIDTH), BF16),
        compiler_params=_params("parallel", "parallel", "arbitrary"),
        name="flash_diff",
    )(qi_arr, ki_arr, qm, kb, vb, *[p.reshape(1, DIFF_DH) for p in lparams], subln.reshape(1, HW))


_PAGES_PER_STEP = 32


def _paged_body(pt_ref, q_ref, *refs, lam_init, n_valid):
    del pt_ref
    pp = _PAGES_PER_STEP
    k_refs, v_refs = refs[:pp], refs[pp:2 * pp]
    (kn_ref, vn_ref, lq1_ref, lk1_ref, lq2_ref, lk2_ref, sub_ref, o_ref,
     m_ref, l_ref, acc_ref, kn_scr, vn_scr) = refs[2 * pp:]
    p_idx = pl.program_id(1)
    nq = q_ref.shape[1]

    @pl.when(p_idx == 0)
    def _():
        m_ref[...] = jnp.full_like(m_ref, NEG)
        l_ref[...] = jnp.zeros_like(l_ref)
        acc_ref[...] = jnp.zeros_like(acc_ref)

    def head_q(h):
        hs = slice(h * HW, (h + 1) * HW)
        return jnp.concatenate([q_ref[0, :, hs], q_ref[1, :, hs]], axis=0).astype(BF16)

    def update(k_page, v_page, n_pages, mask):
        qs = [head_q(h) for h in range(HEADS)]
        s = jnp.concatenate(
            [jnp.concatenate([_dot_nt(qs[h], k_page(j, h).astype(BF16)) for j in range(n_pages)], axis=-1)
             for h in range(HEADS)], axis=0)
        if mask is not None:
            s = jnp.where(mask, s, NEG)
        m_prev = m_ref[...]
        m_new = jnp.maximum(m_prev, jnp.max(s, axis=-1, keepdims=True))
        alpha = jnp.exp(m_prev - m_new)
        p = jnp.exp(s - m_new).astype(BF16)
        l_ref[...] = alpha * l_ref[...] + jnp.sum(p.astype(F32), axis=-1, keepdims=True)
        pvs = []
        for h in range(HEADS):
            ph = p[h * 2 * nq:(h + 1) * 2 * nq]
            pv = _dot(ph[:, 0:PAGE], v_page(0, h).astype(BF16))
            for j in range(1, n_pages):
                pv = pv + _dot(ph[:, j * PAGE:(j + 1) * PAGE], v_page(j, h).astype(BF16))
            pvs.append(pv)
        acc_ref[...] = alpha * acc_ref[...] + jnp.concatenate(pvs, axis=0)
        m_ref[...] = m_new

    head_rows = lambda h: pl.ds(h, PAGE, stride=HEADS)
    update(lambda j, h: k_refs[j][head_rows(h), :], lambda j, h: v_refs[j][head_rows(h), :], pp, None)

    @pl.when(p_idx == pl.num_programs(1) - 1)
    def _():
        kn_scr[...] = jnp.zeros_like(kn_scr)
        vn_scr[...] = jnp.zeros_like(vn_scr)
        kn_scr[0:nq, :] = kn_ref[...]
        vn_scr[0:nq, :] = vn_ref[...]
        ri = lax.broadcasted_iota(jnp.int32, (HEADS * 2 * nq, PAGE), 0) % nq
        ci = lax.broadcasted_iota(jnp.int32, (HEADS * 2 * nq, PAGE), 1)
        update(lambda j, h: kn_scr[:, h * HW:(h + 1) * HW], lambda j, h: vn_scr[:, h * HW:(h + 1) * HW], 1,
               (ci <= ri) & (ci < n_valid))
        lam = _lambda(lq1_ref, lk1_ref, lq2_ref, lk2_ref, lam_init)
        for h in range(HEADS):
            hs = slice(h * HW, (h + 1) * HW)
            r0 = slice(h * 2 * nq, h * 2 * nq + nq)
            r1 = slice(h * 2 * nq + nq, (h + 1) * 2 * nq)
            o = acc_ref[r0] * (1.0 / l_ref[r0]) - lam * (acc_ref[r1] * (1.0 / l_ref[r1]))
            o_ref[:, hs] = (_rms(o, sub_ref[...]) * (1.0 - lam_init)).astype(o_ref.dtype)


def paged_diff(qm, kb, vb, cache_k, cache_v, page_table, layer, lparams, subln, lam_init, n_valid):
    _, b, nq, _ = qm.shape
    n_pages = page_table.shape[1]
    pp = _PAGES_PER_STEP
    assert n_pages % pp == 0
    depth, n_pool = cache_k.shape[:2]
    cache_k = cache_k.reshape(depth, n_pool, PAGE * HEADS, HW)
    cache_v = cache_v.reshape(depth, n_pool, PAGE * HEADS, HW)
    page_spec = lambda j: pl.BlockSpec((None, None, PAGE * HEADS, HW),
                                       lambda i, p, pt: (layer, pt[i, p * pp + j], 0, 0))
    page_specs = [page_spec(j) for j in range(pp)]
    new_spec = pl.BlockSpec((None, nq, WIDTH), lambda i, p, pt: (i, 0, 0))
    lspec = pl.BlockSpec((1, DIFF_DH), lambda i, p, pt: (0, 0))
    grid_spec = pltpu.PrefetchScalarGridSpec(
        num_scalar_prefetch=1,
        grid=(b, n_pages // pp),
        in_specs=([pl.BlockSpec((2, None, nq, WIDTH), lambda i, p, pt: (0, i, 0, 0))]
                  + page_specs + page_specs
                  + [new_spec, new_spec, lspec, lspec, lspec, lspec,
                     pl.BlockSpec((1, HW), lambda i, p, pt: (0, 0))]),
        out_specs=pl.BlockSpec((None, nq, WIDTH), lambda i, p, pt: (i, 0, 0)),
        scratch_shapes=[pltpu.VMEM((HEADS * 2 * nq, 1), F32), pltpu.VMEM((HEADS * 2 * nq, 1), F32),
                        pltpu.VMEM((HEADS * 2 * nq, HW), F32),
                        pltpu.VMEM((PAGE, WIDTH), F32), pltpu.VMEM((PAGE, WIDTH), F32)],
    )
    return pl.pallas_call(
        functools.partial(_paged_body, lam_init=lam_init, n_valid=n_valid),
        grid_spec=grid_spec,
        out_shape=jax.ShapeDtypeStruct((b, nq, WIDTH), F32),
        compiler_params=_params("parallel", "arbitrary"),
        name="paged_diff",
    )(page_table, qm, *([cache_k] * pp), *([cache_v] * pp), kb, vb,
      *[p.reshape(1, DIFF_DH) for p in lparams], subln.reshape(1, HW))


def _gelu(x):
    return 0.5 * x * (1.0 + jnp.tanh(math.sqrt(2.0 / math.pi) * (x + 0.044715 * (x * x * x))))


def _gmlp_body(u_ref, v_ref, lg_ref, lb_ref, ws_ref, bst_ref, o_ref, gv_ref, v_scr, *, tr):
    u = _gelu(u_ref[...].astype(F32))
    v = _gelu(v_ref[...])
    mu = jnp.mean(v, axis=-1, keepdims=True)
    vc = v - mu
    v = vc * lax.rsqrt(jnp.mean(vc * vc, axis=-1, keepdims=True) + EPS) * lg_ref[...] + lb_ref[...]
    gv_ref[...] = v
    if tr < GMLP_CHUNK:
        v_scr[...] = jnp.zeros_like(v_scr)
        v_scr[0:tr, :] = v
    ii = lax.broadcasted_iota(jnp.int32, (tr, GMLP_CHUNK), 0)
    jj = lax.broadcasted_iota(jnp.int32, (tr, GMLP_CHUNK), 1)
    for g in range(HEADS):
        gs = slice(g * HW, (g + 1) * HW)
        w = jnp.where(ii >= jj, ws_ref[g, 0:tr, :], 0.0).astype(BF16)
        bias = bst_ref[0:tr, g:g + 1]
        for r0 in range(0, u.shape[0], tr):
            vg = (v_scr[:, gs] if tr < GMLP_CHUNK else v[r0:r0 + tr, gs]).astype(BF16)
            o_ref[r0:r0 + tr, gs] = (u[r0:r0 + tr, gs] * (_dot(w, vg) + bias)).astype(o_ref.dtype)


def gmlp(za, zb, ln_g, ln_b, ws, bs, tr, act_dtype):
    t = zb.shape[0]
    tm = _tile(t, 4 * tr) if tr == GMLP_CHUNK else tr
    zspec = lambda off: pl.BlockSpec((tm, WIDTH), lambda i: (i, off // WIDTH))
    ospec = pl.BlockSpec((tm, WIDTH), lambda i: (i, 0))
    vec = pl.BlockSpec((1, WIDTH), lambda i: (0, 0))
    return pl.pallas_call(
        functools.partial(_gmlp_body, tr=tr),
        grid=(t // tm,),
        in_specs=[zspec(_ZA_CU), zspec(_ZB_CV), vec, vec,
                  pl.BlockSpec((HEADS, GMLP_CHUNK, GMLP_CHUNK), lambda i: (0, 0, 0)),
                  pl.BlockSpec((GMLP_CHUNK, HEADS), lambda i: (0, 0))],
        out_specs=[ospec, ospec],
        out_shape=[jax.ShapeDtypeStruct((t, WIDTH), act_dtype), jax.ShapeDtypeStruct((t, WIDTH), F32)],
        scratch_shapes=[pltpu.VMEM((GMLP_CHUNK, WIDTH), F32)],
        compiler_params=_params("parallel"),
        name="gmlp",
    )(za, zb, ln_g.reshape(1, WIDTH), ln_b.reshape(1, WIDTH), ws, bs.T)


def _merge_body(x_ref, gate_ref, oa_ref, ob_ref, oc_ref, wa_ref, wb_ref, wc_ref, wo_ref, o_ref):
    d = D_MODEL
    gate = lambda i: _sigmoid(gate_ref[:, i * d:(i + 1) * d].astype(F32))
    merged = (gate(0) * _dot(oa_ref[...].astype(BF16), wa_ref[...])
              + gate(1) * _dot(ob_ref[...].astype(BF16), wb_ref[...])
              + gate(2) * _dot(oc_ref[...].astype(BF16), wc_ref[...]))
    o_ref[...] = x_ref[...] + _dot(merged.astype(BF16), wo_ref[...])


def merge(x, z, oa, ob, oc, wa, wb, wc, wo, tm):
    t, d = x.shape
    row = lambda w: pl.BlockSpec((tm, w), lambda i: (i, 0))
    full = lambda a: pl.BlockSpec(a.shape, lambda i: (0, 0))
    return pl.pallas_call(
        _merge_body,
        grid=(t // tm,),
        in_specs=[row(d), row(3 * d), row(WIDTH), row(WIDTH), row(WIDTH),
                  full(wa), full(wb), full(wc), full(wo)],
        out_specs=row(d),
        out_shape=jax.ShapeDtypeStruct((t, d), F32),
        compiler_params=_params("parallel"),
        name="merge",
    )(x, z, oa, ob, oc, wa, wb, wc, wo)


def _xattn_body(x_ref, g_ref, wq_ref, mk_ref, mv_ref, wo_ref, o_ref, *, interleaved):
    x = x_ref[...]
    q = _dot(_rms(x, g_ref[...]).astype(BF16), wq_ref[...])
    if interleaved:
        m = mk_ref.shape[0] // HEADS
        head = lambda ref, h: ref[pl.ds(h, m, stride=HEADS), :]
    else:
        head = lambda ref, h: ref[:, h * HW:(h + 1) * HW]
    ss = [_dot_nt(q[:, h * HW:(h + 1) * HW].astype(BF16), head(mk_ref, h).astype(BF16)) * (HW ** -0.5)
          for h in range(HEADS)]
    es = [jnp.exp(s - jnp.max(s, axis=-1, keepdims=True)) for s in ss]
    ps = [(e * (1.0 / jnp.sum(e, axis=-1, keepdims=True))).astype(BF16) for e in es]
    o = jnp.concatenate([_dot(p, head(mv_ref, h).astype(BF16)) for h, p in enumerate(ps)], axis=-1)
    o_ref[...] = x + _dot(o.astype(BF16), wo_ref[...])


def xattn(x, g, wq, mem_k, mem_v, layer, wo, tm, tiles_per_seq):
    t, d = x.shape
    interleaved = mem_k.ndim == 5
    if interleaved:
        mem_k = mem_k.reshape(mem_k.shape[:2] + (mem_k.shape[2] * HEADS, HW))
        mem_v = mem_v.reshape(mem_k.shape)
    mem_spec = pl.BlockSpec((None, None) + mem_k.shape[2:], lambda i: (layer, i // tiles_per_seq, 0, 0))
    return pl.pallas_call(
        functools.partial(_xattn_body, interleaved=interleaved),
        grid=(t // tm,),
        in_specs=[pl.BlockSpec((tm, d), lambda i: (i, 0)),
                  pl.BlockSpec((1, d), lambda i: (0, 0)),
                  pl.BlockSpec(wq.shape, lambda i: (0, 0)),
                  mem_spec, mem_spec,
                  pl.BlockSpec(wo.shape, lambda i: (0, 0))],
        out_specs=pl.BlockSpec((tm, d), lambda i: (i, 0)),
        out_shape=jax.ShapeDtypeStruct((t, d), F32),
        compiler_params=_params("parallel"),
        name="xattn",
    )(x, g.reshape(1, d), wq, mem_k, mem_v, wo)


def _ffn_body(x_ref, g_ref, w1_ref, w2_ref, o_ref, h_ref):
    @pl.when(pl.program_id(1) == 0)
    def _():
        x = x_ref[...]
        h_ref[...] = _rms(x, g_ref[...]).astype(BF16)
        o_ref[...] = x

    a = jnp.maximum(_dot(h_ref[...], w1_ref[...]), 0.0)
    o_ref[...] += _dot((a * a).astype(BF16), w2_ref[...])


def ffn(x, g, w1, w2, tm, tf):
    t, d = x.shape
    f = w1.shape[1]
    return pl.pallas_call(
        _ffn_body,
        grid=(t // tm, f // tf),
        in_specs=[pl.BlockSpec((tm, d), lambda i, j: (i, 0)),
                  pl.BlockSpec((1, d), lambda i, j: (0, 0)),
                  pl.BlockSpec((d, tf), lambda i, j: (0, j)),
                  pl.BlockSpec((tf, d), lambda i, j: (j, 0))],
        out_specs=pl.BlockSpec((tm, d), lambda i, j: (i, 0)),
        out_shape=jax.ShapeDtypeStruct((t, d), F32),
        scratch_shapes=[pltpu.VMEM((tm, d), BF16)],
        compiler_params=_params("parallel", "arbitrary"),
        name="ffn",
    )(x, g.reshape(1, d), w1, w2)


def _final_norm_body(x_ref, g_ref, o_ref):
    o_ref[...] = _rms(x_ref[...], g_ref[...])


def final_norm(x, g, tm):
    t, d = x.shape
    return pl.pallas_call(
        _final_norm_body,
        grid=(t // tm,),
        in_specs=[pl.BlockSpec((tm, d), lambda i: (i, 0)), pl.BlockSpec((1, d), lambda i: (0, 0))],
        out_specs=pl.BlockSpec((tm, d), lambda i: (i, 0)),
        out_shape=jax.ShapeDtypeStruct((t, d), F32),
        compiler_params=_params("parallel"),
        name="final_norm",
    )(x, g.reshape(1, d))


def _regroup_w_in(w):
    o = 0
    parts = {}
    for name, size in (("q", WIDTH), ("k", WIDTH), ("v", WIDTH), ("a", HEADS), ("b", HEADS), ("g", WIDTH),
                       ("dq", WIDTH), ("dk", WIDTH), ("dv", WIDTH), ("cu", WIDTH), ("cv", WIDTH),
                       ("gates", 3 * D_MODEL)):
        parts[name] = w[:, o:o + size]
        o += size
    used = _Z_HALF + _ZB_AB + 2 * HEADS
    cols = [parts[n] for n in ("gates", "g", "cu", "q", "k", "v", "dq", "dk", "dv", "cv", "a", "b")]
    cols.append(jnp.zeros((w.shape[0], 2 * _Z_HALF - used), w.dtype))
    return jnp.concatenate(cols, axis=1).astype(BF16)


def _tile(n, want):
    while n % want:
        want //= 2
    return want


def kernel(x_prompt, x_sample, cache_diff_k, cache_diff_v, page_table, cache_mem_k, cache_mem_v, state_gdn, state_gdn_conv, mem_prompt, w_in, norm_mix, gdn_conv_w, gdn_A_log, gdn_dt_bias, gdn_o_norm, diff_lq1, diff_lk1, diff_lq2, diff_lk2, diff_subln, gmlp_ln_g, gmlp_ln_b, gmlp_ws, gmlp_b, w_br_gdn, w_br_diff, w_br_gmlp, w_out, norm_xa, norm_mem, xa_wq, xa_wk, xa_wv, xa_wo, norm_ffn, ffn_w1, ffn_w2, norm_final):
    depth = w_in.shape[0]
    bp, lp, d = x_prompt.shape
    bs, ls, _ = x_sample.shape
    mem_len = mem_prompt.shape[1]
    past_len = page_table.shape[1] * PAGE
    lpad = SAMPLE_PAD
    bf = lambda a: a.astype(BF16)

    mem_tok = mem_prompt.reshape(bp * mem_len, d)
    mem_kv = [norm_matmul(mem_tok, norm_mem[l], bf(jnp.concatenate([xa_wk[l], xa_wv[l]], axis=1)),
                          _tile(bp * mem_len, 512), 512) for l in range(depth)]
    mem_k_p = jnp.stack([kv[:, :WIDTH] for kv in mem_kv]).reshape(depth, bp, mem_len, WIDTH)
    mem_v_p = jnp.stack([kv[:, WIDTH:] for kv in mem_kv]).reshape(depth, bp, mem_len, WIDTH)

    weights = [dict(w_in=_regroup_w_in(w_in[l]), wa=bf(w_br_gdn[l]), wb=bf(w_br_diff[l]), wc=bf(w_br_gmlp[l]),
                    wo=bf(w_out[l]), xq=bf(xa_wq[l]), xo=bf(xa_wo[l]), w1=bf(ffn_w1[l]), w2=bf(ffn_w2[l]))
               for l in range(depth)]

    def trunk(x, b, l, l_valid, pos, mem_k, mem_v, s_in, buf_in, paged):
        t = b * l
        prompt = paged is None
        act = BF16 if prompt else F32
        tm = _tile(t, 1024)
        tables = _rope_tables(pos)
        outs = dict(s=[], buf=[], gv=[])
        k_all = v_all = None
        for li in range(depth):
            w = weights[li]
            lam_init = 0.8 - 0.6 * math.exp(-0.3 * li)
            lparams = (diff_lq1[li], diff_lk1[li], diff_lq2[li], diff_lk2[li])
            za, zb = proj_in(x, norm_mix[li], w["w_in"], tm, 1024, act)
            za3, zb3 = za.reshape(b, l, _Z_HALF), zb.reshape(b, l, _Z_HALF)
            o_a, s_new = gdn(za3, zb3, buf_in[li], s_in, li, gdn_conv_w[li], gdn_A_log[li], gdn_dt_bias[li],
                             gdn_o_norm[li], _tile(l, 256) if prompt else GDN_CHUNK, _tile(l, 256) if prompt else l,
                             l_valid, act)
            outs["buf"].append(zb3[:, l_valid - (GDN_CONV - 1):l_valid, _ZB_QKV:_ZB_QKV + GDN_CONV_CH])
            qm, k_all, kb, v_all, vb = rope_split(zb, tables, _tile(l, 512), act, li, depth, k_all, v_all)
            if prompt:
                o_b = flash_diff(qm, kb, vb, lparams, diff_subln[li], lam_init, b, _tile(l, 2048), _tile(l, 1024))
            else:
                ck, cv, pt = paged
                o_b = paged_diff(qm.reshape(2, b, l, WIDTH), kb.reshape(b, l, WIDTH), vb.reshape(b, l, WIDTH),
                                 ck, cv, pt, li, lparams, diff_subln[li], lam_init, l_valid)
                o_b = o_b.reshape(t, WIDTH)
            o_c, gv = gmlp(za, zb, gmlp_ln_g[li], gmlp_ln_b[li], gmlp_ws[li], gmlp_b[li], min(l, GMLP_CHUNK), act)
            x = merge(x, za, o_a.reshape(t, WIDTH), o_b, o_c, w["wa"], w["wb"], w["wc"], w["wo"], _tile(t, 512))
            xa_tm = _tile(l, 512)
            x = xattn(x, norm_xa[li], w["xq"], mem_k, mem_v, li, w["xo"], xa_tm, l // xa_tm)
            x = ffn(x, norm_ffn[li], w["w1"], w["w2"], tm, 1024)
            outs["s"].append(s_new)
            outs["gv"].append(gv)
        y = final_norm(x, norm_final, tm)
        seq = lambda a: a.reshape((depth, b, l) + a.shape[2:])[:, :, :l_valid]
        return (y.reshape(b, l, d)[:, :l_valid], seq(k_all), seq(v_all),
                jnp.stack(outs["s"]), jnp.stack(outs["buf"]), seq(jnp.stack(outs["gv"])))

    s0_p = jnp.zeros((depth, bp, HEADS, HW, HW), F32)
    buf0_p = jnp.zeros((depth, bp, GDN_CONV - 1, GDN_CONV_CH), F32)
    (y_p, k_p, v_p, s_p, buf_p, _) = trunk(x_prompt.reshape(bp * lp, d), bp, lp, lp, jnp.arange(lp),
                                           mem_k_p, mem_v_p, s0_p, buf0_p, None)

    x_s = jnp.pad(x_sample, ((0, 0), (0, lpad - ls), (0, 0))).reshape(bs * lpad, d)
    (y_s, k_s, v_s, s_s, buf_s, gv_s) = trunk(x_s, bs, lpad, ls, past_len + jnp.arange(lpad),
                                              cache_mem_k, cache_mem_v, state_gdn, state_gdn_conv,
                                              (cache_diff_k, cache_diff_v, page_table))

    return (y_p, y_s, k_p, v_p,
            mem_k_p.reshape(depth, bp, mem_len, HEADS, HW), mem_v_p.reshape(depth, bp, mem_len, HEADS, HW),
            s_p, buf_p, k_s, v_s, s_s, buf_s, gv_s)
```

```python
import functools
import math

import jax
import jax.numpy as jnp
from jax import lax
from jax.experimental import pallas as pl
from jax.experimental.pallas import tpu as pltpu

F32 = jnp.float32
BF16 = jnp.bfloat16
EPS = 1e-6
NEG = -1e30

D_MODEL = 1024
HEADS = 4
HW = 128
WIDTH = HEADS * HW
DIFF_DH = 64
ROPE_DIM = 16
ROPE_THETA = 500000.0
GDN_CONV = 4
GDN_CHUNK = 64
GDN_CONV_CH = 3 * WIDTH
GMLP_CHUNK = 128
PAGE = 128
SAMPLE_PAD = 8

_ZA_GATES = 0
_ZA_GIN = 3 * D_MODEL
_ZA_CU = _ZA_GIN + WIDTH
_ZB_QKV = 0
_ZB_DQ = GDN_CONV_CH
_ZB_DK = _ZB_DQ + WIDTH
_ZB_DV = _ZB_DK + WIDTH
_ZB_CV = _ZB_DV + WIDTH
_ZB_AB = _ZB_CV + WIDTH
_Z_HALF = 4096

_VMEM_LIMIT = 56 * 1024 * 1024


def _params(*sem):
    return pltpu.CompilerParams(dimension_semantics=sem, vmem_limit_bytes=_VMEM_LIMIT)


def _rms(x, g):
    return x * lax.rsqrt(jnp.mean(x * x, axis=-1, keepdims=True) + EPS) * g


def _sigmoid(x):
    return 1.0 / (1.0 + jnp.exp(-x))


def _dot(a, b):
    return jnp.dot(a, b, preferred_element_type=F32)


def _dot_nt(a, b):
    return lax.dot_general(a, b, (((1,), (1,)), ((), ())), preferred_element_type=F32)


def _dot_tn(a, b):
    return lax.dot_general(a, b, (((0,), (0,)), ((), ())), preferred_element_type=F32)


def _split(a):
    hi = lax.bitcast_convert_type(lax.bitcast_convert_type(a, jnp.int32) & jnp.int32(-65536), F32)
    return hi, a - hi


def _hi_lhs(a):
    hi, lo = _split(a)
    return jnp.concatenate([hi, hi, lo], axis=1).astype(BF16)


def _hi_rhs(b):
    hi, lo = _split(b)
    return jnp.concatenate([hi, lo, hi], axis=0).astype(BF16)


def _dot_hi(a, b):
    return _dot(_hi_lhs(a), _hi_rhs(b))


def _dot_hi_exact_lhs(a_bf16, b):
    hi, lo = _split(b)
    lo_hi, lo_lo = _split(lo)
    return _dot(jnp.concatenate([a_bf16] * 3, axis=1), jnp.concatenate([hi, lo_hi, lo_lo], axis=0).astype(BF16))


def _dot_hi_exact_rhs(a, b_bf16):
    hi, lo = _split(a)
    lo_hi, lo_lo = _split(lo)
    return _dot(jnp.concatenate([hi, lo_hi, lo_lo], axis=1).astype(BF16), jnp.concatenate([b_bf16] * 3, axis=0))


def _norm_mm_body(x_ref, g_ref, w_ref, o_ref, h_ref):
    @pl.when(pl.program_id(1) == 0)
    def _():
        h_ref[...] = _rms(x_ref[...], g_ref[...]).astype(BF16)

    o_ref[...] = _dot(h_ref[...], w_ref[...]).astype(o_ref.dtype)


def norm_matmul(x, g, w, tm, tn):
    t, d = x.shape
    n = w.shape[1]
    return pl.pallas_call(
        _norm_mm_body,
        grid=(t // tm, n // tn),
        in_specs=[pl.BlockSpec((tm, d), lambda i, j: (i, 0)),
                  pl.BlockSpec((1, d), lambda i, j: (0, 0)),
                  pl.BlockSpec((d, tn), lambda i, j: (0, j))],
        out_specs=pl.BlockSpec((tm, tn), lambda i, j: (i, j)),
        out_shape=jax.ShapeDtypeStruct((t, n), F32),
        scratch_shapes=[pltpu.VMEM((tm, d), BF16)],
        compiler_params=_params("parallel", "arbitrary"),
        name="norm_matmul",
    )(x, g.reshape(1, d), w)


def _proj_in_body(x_ref, g_ref, w_ref, oa_ref, ob_ref, *, tn):
    h = _rms(x_ref[...], g_ref[...]).astype(BF16)
    for j in range(_Z_HALF // tn):
        oa_ref[:, j * tn:(j + 1) * tn] = _dot(h, w_ref[:, j * tn:(j + 1) * tn]).astype(oa_ref.dtype)
    for j in range(_Z_HALF // tn):
        ob_ref[:, j * tn:(j + 1) * tn] = _dot(h, w_ref[:, _Z_HALF + j * tn:_Z_HALF + (j + 1) * tn])


def proj_in(x, g, w_all, layer, tm, tn, act_dtype):
    t, d = x.shape
    return pl.pallas_call(
        functools.partial(_proj_in_body, tn=tn),
        grid=(t // tm,),
        in_specs=[pl.BlockSpec((tm, d), lambda i: (i, 0)),
                  pl.BlockSpec((1, d), lambda i: (0, 0)),
                  pl.BlockSpec((None, d, 2 * _Z_HALF), lambda i: (layer, 0, 0), pipeline_mode=pl.Buffered(1))],
        out_specs=[pl.BlockSpec((tm, _Z_HALF), lambda i: (i, 0)),
                   pl.BlockSpec((tm, _Z_HALF), lambda i: (i, 0))],
        out_shape=[jax.ShapeDtypeStruct((t, _Z_HALF), act_dtype), jax.ShapeDtypeStruct((t, _Z_HALF), F32)],
        compiler_params=_params("parallel"),
        name="proj_in",
    )(x, g.reshape(1, d), w_all)


def _gdn_body(qkv_ref, gin_ref, ab_ref, buf_ref, s0_ref, cw_ref, alog_ref, dtb_ref, onorm_ref,
              o_ref, sout_ref, xs_ref, ab_scr, s_ref, *, tl, tl_in, valid):
    c = GDN_CHUNK
    nc = tl // c
    t = pl.program_id(1)

    @pl.when(t == 0)
    def _():
        s_ref[...] = s0_ref[...]
        xs_ref[...] = jnp.zeros_like(xs_ref)
        xs_ref[5:8, :] = buf_ref[...]
        ab_scr[...] = jnp.zeros_like(ab_scr)

    xs_ref[8:8 + tl_in, :] = qkv_ref[...]
    ab_scr[0:tl_in, :] = ab_ref[...]
    w = cw_ref[...]
    y = (xs_ref[5:5 + tl, :] * w[0:1] + xs_ref[6:6 + tl, :] * w[1:2]
         + xs_ref[7:7 + tl, :] * w[2:3] + xs_ref[8:8 + tl, :] * w[3:4])
    xs_ref[5:8, :] = xs_ref[tl + 5:tl + 8, :]
    y = y * _sigmoid(y)

    ab = ab_scr[...]
    sp_in = ab + dtb_ref[...]
    softplus = jnp.maximum(sp_in, 0.0) + jnp.log1p(jnp.exp(-jnp.abs(sp_in)))
    g = -jnp.exp(alog_ref[...]) * softplus
    beta = _sigmoid(ab)
    if valid < tl:
        live = lax.broadcasted_iota(jnp.int32, (tl, 1), 0) < valid
        g = jnp.where(live, g, 0.0)
        beta = jnp.where(live, beta, 0.0)

    ti = lax.broadcasted_iota(jnp.int32, (tl, tl), 0)
    tj = lax.broadcasted_iota(jnp.int32, (tl, tl), 1)
    same_chunk = (ti // c) == (tj // c)
    gcol = _dot_hi_exact_lhs(jnp.where(same_chunk & (ti >= tj), 1.0, 0.0).astype(BF16), g)
    grow = _dot_hi_exact_rhs(g.T[0:8, :], jnp.where(same_chunk & (ti <= tj), 1.0, 0.0).astype(BF16))

    ii = lax.broadcasted_iota(jnp.int32, (c, c), 0)
    jj = lax.broadcasted_iota(jnp.int32, (c, c), 1)
    incl = ii >= jj
    strict = ii > jj
    eye = (ii == jj).astype(F32)

    def prepare(ci, h):
        rs = slice(ci * c, (ci + 1) * c)
        gi = gcol[rs, h:h + 1]
        gj = grow[h:h + 1, rs]
        dec_incl = jnp.exp(jnp.where(incl, gi - gj, NEG))
        glast = gcol[(ci + 1) * c - 1:(ci + 1) * c, h:h + 1]
        qh = y[rs, h * HW:(h + 1) * HW]
        kh = y[rs, WIDTH + h * HW:WIDTH + (h + 1) * HW]
        qh = qh * lax.rsqrt(jnp.sum(qh * qh, axis=-1, keepdims=True) + EPS) * (HW ** -0.5)
        kh = kh * lax.rsqrt(jnp.sum(kh * kh, axis=-1, keepdims=True) + EPS)
        return dict(dec_incl=dec_incl, dec_strict=jnp.where(strict, dec_incl, 0.0), gexp=jnp.exp(gi),
                    bcol=beta[rs, 4 + h:5 + h], kh=kh, vh=y[rs, 2 * WIDTH + h * HW:2 * WIDTH + (h + 1) * HW],
                    qb=qh.astype(BF16), kb=kh.astype(BF16), tail=jnp.exp(glast - gi), gend=jnp.exp(glast))

    terms = [prepare(ci, h) for ci in range(nc) for h in range(HEADS)]
    for m in terms:
        kk = _dot_nt(m["kb"], m["kb"])
        m["p"] = (_dot_nt(m["qb"], m["kb"]) * m["dec_incl"]).astype(BF16)
        m["x"] = -(m["bcol"] * kk * m["dec_strict"])
        m["tinv"] = eye + m["x"]
    for m in terms:
        m["x"] = _dot_hi(m["x"], m["x"])
    for _ in range(int(math.log2(c)) - 2):
        for m in terms:
            r = _dot_hi(jnp.concatenate([m["x"], m["tinv"]], axis=0), m["x"])
            m["x"] = r[0:c]
            m["tinv"] = m["tinv"] + r[c:2 * c]
    for m in terms:
        m["tinv"] = m["tinv"] + _dot_hi(m["tinv"], m["x"])
    for m in terms:
        sol = _dot_hi(m["tinv"], jnp.concatenate([m["bcol"] * m["vh"], (m["bcol"] * m["gexp"]) * m["kh"]], axis=-1))
        m["u0"] = sol[:, :HW]
        m["w"] = sol[:, HW:].astype(BF16)

    state = [s_ref[h] for h in range(HEADS)]
    for ci in range(nc):
        ms = terms[ci * HEADS:(ci + 1) * HEADS]
        sbs = [s.astype(BF16) for s in state]
        us = [m["u0"] - _dot_nt(m["w"], sb) for m, sb in zip(ms, sbs)]
        qs = [_dot_nt(m["qb"], sb) for m, sb in zip(ms, sbs)]
        pus = [_dot(m["p"], u.astype(BF16)) for m, u in zip(ms, us)]
        state = [m["gend"] * s + _dot_tn((u * m["tail"]).astype(BF16), m["kb"])
                 for m, s, u in zip(ms, state, us)]
        r0 = ci * c
        rows = min(c, tl_in - r0)
        if rows > 0:
            for h, (m, qs_h, pu) in enumerate(zip(ms, qs, pus)):
                hs = slice(h * HW, (h + 1) * HW)
                o = _rms(m["gexp"] * qs_h + pu, onorm_ref[...])[0:rows]
                gin = gin_ref[r0:r0 + rows, hs].astype(F32)
                o_ref[r0:r0 + rows, hs] = (o * (gin * _sigmoid(gin))).astype(o_ref.dtype)
    for h in range(HEADS):
        s_ref[h] = state[h]

    @pl.when(t == pl.num_programs(1) - 1)
    def _():
        sout_ref[...] = s_ref[...]


def gdn(za3, zb3, conv_buf, s0, layer, conv_w, a_log, dt_bias, o_norm, tl, tl_in, valid, act_dtype):
    b, l, _ = zb3.shape
    nt = l // tl_in
    assert tl % GDN_CHUNK == 0 and (tl_in == tl or nt == 1)
    lane_pad = lambda v, off: jnp.zeros((1, HW), F32).at[0, off:off + HEADS].set(v)
    body = functools.partial(_gdn_body, tl=tl, tl_in=tl_in, valid=valid)
    return pl.pallas_call(
        body,
        grid=(b, nt),
        in_specs=[pl.BlockSpec((None, tl_in, GDN_CONV_CH), lambda i, t: (i, t, _ZB_QKV // GDN_CONV_CH)),
                  pl.BlockSpec((None, tl_in, WIDTH), lambda i, t: (i, t, _ZA_GIN // WIDTH)),
                  pl.BlockSpec((None, tl_in, HW), lambda i, t: (i, t, _ZB_AB // HW)),
                  pl.BlockSpec((None, GDN_CONV - 1, GDN_CONV_CH), lambda i, t: (i, 0, 0)),
                  pl.BlockSpec((None, None, HEADS, HW, HW), lambda i, t: (layer, i, 0, 0, 0)),
                  pl.BlockSpec((GDN_CONV, GDN_CONV_CH), lambda i, t: (0, 0)),
                  pl.BlockSpec((1, HW), lambda i, t: (0, 0)),
                  pl.BlockSpec((1, HW), lambda i, t: (0, 0)),
                  pl.BlockSpec((1, HW), lambda i, t: (0, 0))],
        out_specs=[pl.BlockSpec((None, tl_in, WIDTH), lambda i, t: (i, t, 0)),
                   pl.BlockSpec((None, HEADS, HW, HW), lambda i, t: (i, 0, 0, 0))],
        out_shape=[jax.ShapeDtypeStruct((b, l, WIDTH), act_dtype),
                   jax.ShapeDtypeStruct((b, HEADS, HW, HW), F32)],
        scratch_shapes=[pltpu.VMEM((tl + 8, GDN_CONV_CH), F32),
                        pltpu.VMEM((tl, HW), F32),
                        pltpu.VMEM((HEADS, HW, HW), F32)],
        compiler_params=_params("parallel", "arbitrary"),
        name="gdn",
    )(zb3, za3, zb3, conv_buf, s0, conv_w, lane_pad(a_log, 0), lane_pad(dt_bias, 0), o_norm.reshape(1, HW))


def _rope_tables(pos):
    half = ROPE_DIM // 2
    inv = ROPE_THETA ** (-jnp.arange(0, ROPE_DIM, 2, dtype=F32) / ROPE_DIM)
    ang = pos.astype(F32)[:, None] * inv[None, :]
    m = jnp.arange(HW) % DIFF_DH
    cos = jnp.cos(ang)[:, m % half]
    sin = jnp.sin(ang)[:, m % half]
    cos_t = jnp.where(m < ROPE_DIM, cos, 1.0)
    sin_a = jnp.where(m < half, -sin, 0.0)
    sin_b = jnp.where((m >= half) & (m < ROPE_DIM), sin, 0.0)
    return cos_t.astype(F32), sin_a.astype(F32), sin_b.astype(F32)


def _rope_body(q_ref, k_ref, v_ref, cos_ref, sa_ref, sb_ref, *refs):
    qm_ref, kr_ref, kb_ref, vf_ref, vb_ref = refs[-5:]
    half = ROPE_DIM // 2
    cos, sa, sb = cos_ref[...], sa_ref[...], sb_ref[...]
    lane = lax.broadcasted_iota(jnp.int32, (1, HW), 1)
    first_map = lane < DIFF_DH

    def rot(x):
        return x * cos + pltpu.roll(x, HW - half, 1) * sa + pltpu.roll(x, half, 1) * sb

    for h in range(HEADS):
        hs = slice(h * HW, (h + 1) * HW)
        q = rot(q_ref[:, hs]) * (DIFF_DH ** -0.5)
        qm_ref[0, :, hs] = jnp.where(first_map, q, 0.0).astype(qm_ref.dtype)
        qm_ref[1, :, hs] = jnp.where(first_map, 0.0, q).astype(qm_ref.dtype)
        k = rot(k_ref[:, hs])
        kr_ref[:, h, :] = k
        kb_ref[:, hs] = k.astype(kb_ref.dtype)
        vf_ref[:, h, :] = v_ref[:, hs]
    vb_ref[...] = v_ref[...].astype(vb_ref.dtype)


def rope_split(z, tables, tr, act_dtype, layer, depth, k_all, v_all):
    t = z.shape[0]
    nl = tables[0].shape[0] // tr
    zspec = lambda off: pl.BlockSpec((tr, WIDTH), lambda i: (i, off // WIDTH))
    tspec = pl.BlockSpec((tr, HW), lambda i: (i % nl, 0))
    ospec = pl.BlockSpec((tr, WIDTH), lambda i: (i, 0))
    hspec = pl.BlockSpec((None, tr, HEADS, HW), lambda i: (layer, i, 0, 0))
    carried = [] if k_all is None else [k_all, v_all]
    return pl.pallas_call(
        _rope_body,
        grid=(t // tr,),
        in_specs=[zspec(_ZB_DQ), zspec(_ZB_DK), zspec(_ZB_DV), tspec, tspec, tspec]
                 + [pl.BlockSpec(memory_space=pl.ANY)] * len(carried),
        out_specs=[pl.BlockSpec((2, tr, WIDTH), lambda i: (0, i, 0)), hspec, ospec, hspec, ospec],
        out_shape=[jax.ShapeDtypeStruct((2, t, WIDTH), act_dtype),
                   jax.ShapeDtypeStruct((depth, t, HEADS, HW), F32),
                   jax.ShapeDtypeStruct((t, WIDTH), act_dtype),
                   jax.ShapeDtypeStruct((depth, t, HEADS, HW), F32),
                   jax.ShapeDtypeStruct((t, WIDTH), act_dtype)],
        input_output_aliases={6: 1, 7: 3} if carried else {},
        compiler_params=_params("parallel"),
        name="rope_split",
    )(z, z, z, *tables, *carried)


def _lambda(lq1_ref, lk1_ref, lq2_ref, lk2_ref, lam_init):
    s1 = jnp.sum(lq1_ref[...] * lk1_ref[...], axis=-1, keepdims=True)
    s2 = jnp.sum(lq2_ref[...] * lk2_ref[...], axis=-1, keepdims=True)
    return jnp.exp(s1) - jnp.exp(s2) + lam_init


_FLASH_ROWS = 256


def _flash_body(qi_ref, ki_ref, q_ref, k_ref, v_ref, lq1_ref, lk1_ref, lq2_ref, lk2_ref, sub_ref, o_ref,
                m_ref, acc_ref, *, lam_init):
    pair = pl.program_id(2)
    qi = qi_ref[pair]
    ki = ki_ref[pair]
    tq = q_ref.shape[1]
    tk = k_ref.shape[0]
    ratio = tq // tk
    rows = min(tq, _FLASH_ROWS)

    @pl.when(ki == 0)
    def _():
        m_ref[...] = jnp.full_like(m_ref, NEG)
        acc_ref[...] = jnp.zeros_like(acc_ref)

    def block(diag):
        k = k_ref[...]
        v = v_ref[...]
        v1 = jnp.concatenate([v, jnp.ones_like(v)], axis=1)
        col0 = 0 if diag is None else diag * tk
        chains = [(c, r0) for c in range(2) for r0 in range(0, tq, rows) if r0 + rows > col0]

        def n_keys(r0):
            return tk if diag is None else min(tk, r0 + rows - col0)

        def scores(c, r0):
            kc = n_keys(r0)
            s = _dot_nt(q_ref[c, r0:r0 + rows, :], k[0:kc])
            if diag is not None and r0 < col0 + kc - 1:
                ri = lax.broadcasted_iota(jnp.int32, s.shape, 0) + r0
                ci = lax.broadcasted_iota(jnp.int32, s.shape, 1) + col0
                s = jnp.where(ci <= ri, s, NEG)
            return s

        ahead = 3
        pending = [scores(*ch) for ch in chains[:ahead]]
        for i, (c, r0) in enumerate(chains):
            rs = slice(r0, r0 + rows)
            s = pending.pop(0)
            kc = n_keys(r0)
            tiles = [s[:, j * HW:(j + 1) * HW] for j in range(kc // HW)]
            tile_max = functools.reduce(jnp.maximum, tiles)
            m_prev = m_ref[c, rs, :]
            m_new = jnp.maximum(m_prev, jnp.max(tile_max, axis=-1, keepdims=True))
            alpha = jnp.exp(m_prev - m_new)
            p = jnp.concatenate([jnp.exp(tl - m_new).astype(BF16) for tl in tiles], axis=1)
            pv = _dot(p, v1[0:kc])
            if i + ahead < len(chains):
                pending.append(scores(*chains[i + ahead]))
            acc_ref[c, rs, :] = jnp.concatenate([alpha, alpha], axis=1) * acc_ref[c, rs, :] + pv
            m_ref[c, rs, :] = m_new

    pl.when(ki < qi * ratio)(functools.partial(block, None))
    for d in range(ratio):
        pl.when(ki == qi * ratio + d)(functools.partial(block, d))

    @pl.when(ki == qi * ratio + ratio - 1)
    def _():
        lam = _lambda(lq1_ref, lk1_ref, lq2_ref, lk2_ref, lam_init)
        o = (acc_ref[0, :, 0:HW] * (1.0 / acc_ref[0, :, HW:2 * HW])
             - lam * (acc_ref[1, :, 0:HW] * (1.0 / acc_ref[1, :, HW:2 * HW])))
        o_ref[...] = (_rms(o, sub_ref[...]) * (1.0 - lam_init)).astype(o_ref.dtype)


def flash_diff(qm, kb, vb, lparams, subln, lam_init, batch, tq, tk):
    t = kb.shape[0]
    l = t // batch
    nq = l // tq
    nk = l // tk
    assert tq % tk == 0
    pairs = [(i, j) for i in range(nq) for j in range((i + 1) * (tq // tk))]
    qi_arr = jnp.asarray([p[0] for p in pairs], jnp.int32)
    ki_arr = jnp.asarray([p[1] for p in pairs], jnp.int32)
    lspec = pl.BlockSpec((1, DIFF_DH), lambda b, h, p, qi, ki: (0, 0))
    kvspec = pl.BlockSpec((tk, HW), lambda b, h, p, qi, ki: (b * nk + ki[p], h))
    grid_spec = pltpu.PrefetchScalarGridSpec(
        num_scalar_prefetch=2,
        grid=(batch, HEADS, len(pairs)),
        in_specs=[pl.BlockSpec((2, tq, HW), lambda b, h, p, qi, ki: (0, b * nq + qi[p], h)),
                  kvspec, kvspec, lspec, lspec, lspec, lspec,
                  pl.BlockSpec((1, HW), lambda b, h, p, qi, ki: (0, 0))],
        out_specs=pl.BlockSpec((tq, HW), lambda b, h, p, qi, ki: (b * nq + qi[p], h)),
        scratch_shapes=[pltpu.VMEM((2, tq, HW), F32), pltpu.VMEM((2, tq, 2 * HW), F32)],
    )
    return pl.pallas_call(
        functools.partial(_flash_body, lam_init=lam_init),
        grid_spec=grid_spec,
        out_shape=jax.ShapeDtypeStruct((t, WIDTH), BF16),
        compiler_params=_params("parallel", "parallel", "arbitrary"),
        name="flash_diff",
    )(qi_arr, ki_arr, qm, kb, vb, *[p.reshape(1, DIFF_DH) for p in lparams], subln.reshape(1, HW))


_PAGES_PER_STEP = 32


def _paged_body(pt_ref, q_ref, *refs, lam_init, n_valid):
    del pt_ref
    pp = _PAGES_PER_STEP
    k_refs, v_refs = refs[:pp], refs[pp:2 * pp]
    (kn_ref, vn_ref, lq1_ref, lk1_ref, lq2_ref, lk2_ref, sub_ref, o_ref,
     m_ref, l_ref, acc_ref, kn_scr, vn_scr) = refs[2 * pp:]
    p_idx = pl.program_id(1)
    nq = q_ref.shape[1]

    @pl.when(p_idx == 0)
    def _():
        m_ref[...] = jnp.full_like(m_ref, NEG)
        l_ref[...] = jnp.zeros_like(l_ref)
        acc_ref[...] = jnp.zeros_like(acc_ref)

    def head_q(h):
        hs = slice(h * HW, (h + 1) * HW)
        return jnp.concatenate([q_ref[0, :, hs], q_ref[1, :, hs]], axis=0).astype(BF16)

    def update(k_page, v_page, n_pages, mask):
        qs = [head_q(h) for h in range(HEADS)]
        s = jnp.concatenate(
            [jnp.concatenate([_dot_nt(qs[h], k_page(j, h).astype(BF16)) for j in range(n_pages)], axis=-1)
             for h in range(HEADS)], axis=0)
        if mask is not None:
            s = jnp.where(mask, s, NEG)
        m_prev = m_ref[...]
        m_new = jnp.maximum(m_prev, jnp.max(s, axis=-1, keepdims=True))
        alpha = jnp.exp(m_prev - m_new)
        p = jnp.exp(s - m_new).astype(BF16)
        l_ref[...] = alpha * l_ref[...] + jnp.sum(p.astype(F32), axis=-1, keepdims=True)
        pvs = []
        for h in range(HEADS):
            ph = p[h * 2 * nq:(h + 1) * 2 * nq]
            pv = _dot(ph[:, 0:PAGE], v_page(0, h).astype(BF16))
            for j in range(1, n_pages):
                pv = pv + _dot(ph[:, j * PAGE:(j + 1) * PAGE], v_page(j, h).astype(BF16))
            pvs.append(pv)
        acc_ref[...] = alpha * acc_ref[...] + jnp.concatenate(pvs, axis=0)
        m_ref[...] = m_new

    head_rows = lambda h: pl.ds(h, PAGE, stride=HEADS)
    update(lambda j, h: k_refs[j][head_rows(h), :], lambda j, h: v_refs[j][head_rows(h), :], pp, None)

    @pl.when(p_idx == pl.num_programs(1) - 1)
    def _():
        kn_scr[...] = jnp.zeros_like(kn_scr)
        vn_scr[...] = jnp.zeros_like(vn_scr)
        kn_scr[0:nq, :] = kn_ref[...]
        vn_scr[0:nq, :] = vn_ref[...]
        ri = lax.broadcasted_iota(jnp.int32, (HEADS * 2 * nq, PAGE), 0) % nq
        ci = lax.broadcasted_iota(jnp.int32, (HEADS * 2 * nq, PAGE), 1)
        update(lambda j, h: kn_scr[:, h * HW:(h + 1) * HW], lambda j, h: vn_scr[:, h * HW:(h + 1) * HW], 1,
               (ci <= ri) & (ci < n_valid))
        lam = _lambda(lq1_ref, lk1_ref, lq2_ref, lk2_ref, lam_init)
        for h in range(HEADS):
            hs = slice(h * HW, (h + 1) * HW)
            r0 = slice(h * 2 * nq, h * 2 * nq + nq)
            r1 = slice(h * 2 * nq + nq, (h + 1) * 2 * nq)
            o = acc_ref[r0] * (1.0 / l_ref[r0]) - lam * (acc_ref[r1] * (1.0 / l_ref[r1]))
            o_ref[:, hs] = (_rms(o, sub_ref[...]) * (1.0 - lam_init)).astype(o_ref.dtype)


def paged_diff(qm, kb, vb, cache_k, cache_v, page_table, layer, lparams, subln, lam_init, n_valid):
    _, b, nq, _ = qm.shape
    n_pages = page_table.shape[1]
    pp = _PAGES_PER_STEP
    assert n_pages % pp == 0
    depth, n_pool = cache_k.shape[:2]
    cache_k = cache_k.reshape(depth, n_pool, PAGE * HEADS, HW)
    cache_v = cache_v.reshape(depth, n_pool, PAGE * HEADS, HW)
    page_spec = lambda j: pl.BlockSpec((None, None, PAGE * HEADS, HW),
                                       lambda i, p, pt: (layer, pt[i, p * pp + j], 0, 0))
    page_specs = [page_spec(j) for j in range(pp)]
    new_spec = pl.BlockSpec((None, nq, WIDTH), lambda i, p, pt: (i, 0, 0))
    lspec = pl.BlockSpec((1, DIFF_DH), lambda i, p, pt: (0, 0))
    grid_spec = pltpu.PrefetchScalarGridSpec(
        num_scalar_prefetch=1,
        grid=(b, n_pages // pp),
        in_specs=([pl.BlockSpec((2, None, nq, WIDTH), lambda i, p, pt: (0, i, 0, 0))]
                  + page_specs + page_specs
                  + [new_spec, new_spec, lspec, lspec, lspec, lspec,
                     pl.BlockSpec((1, HW), lambda i, p, pt: (0, 0))]),
        out_specs=pl.BlockSpec((None, nq, WIDTH), lambda i, p, pt: (i, 0, 0)),
        scratch_shapes=[pltpu.VMEM((HEADS * 2 * nq, 1), F32), pltpu.VMEM((HEADS * 2 * nq, 1), F32),
                        pltpu.VMEM((HEADS * 2 * nq, HW), F32),
                        pltpu.VMEM((PAGE, WIDTH), F32), pltpu.VMEM((PAGE, WIDTH), F32)],
    )
    return pl.pallas_call(
        functools.partial(_paged_body, lam_init=lam_init, n_valid=n_valid),
        grid_spec=grid_spec,
        out_shape=jax.ShapeDtypeStruct((b, nq, WIDTH), F32),
        compiler_params=_params("parallel", "arbitrary"),
        name="paged_diff",
    )(page_table, qm, *([cache_k] * pp), *([cache_v] * pp), kb, vb,
      *[p.reshape(1, DIFF_DH) for p in lparams], subln.reshape(1, HW))


def _gelu(x):
    return 0.5 * x * (1.0 + jnp.tanh(math.sqrt(2.0 / math.pi) * (x + 0.044715 * (x * x * x))))


def _gmlp_body(u_ref, v_ref, lg_ref, lb_ref, ws_ref, bst_ref, o_ref, gv_ref, v_scr, *, tr):
    u = _gelu(u_ref[...].astype(F32))
    v = _gelu(v_ref[...])
    mu = jnp.mean(v, axis=-1, keepdims=True)
    vc = v - mu
    v = vc * lax.rsqrt(jnp.mean(vc * vc, axis=-1, keepdims=True) + EPS) * lg_ref[...] + lb_ref[...]
    gv_ref[...] = v
    if tr < GMLP_CHUNK:
        v_scr[...] = jnp.zeros_like(v_scr)
        v_scr[0:tr, :] = v
    ii = lax.broadcasted_iota(jnp.int32, (tr, GMLP_CHUNK), 0)
    jj = lax.broadcasted_iota(jnp.int32, (tr, GMLP_CHUNK), 1)
    for g in range(HEADS):
        gs = slice(g * HW, (g + 1) * HW)
        w = jnp.where(ii >= jj, ws_ref[g, 0:tr, :], 0.0).astype(BF16)
        bias = bst_ref[0:tr, g:g + 1]
        for r0 in range(0, u.shape[0], tr):
            vg = (v_scr[:, gs] if tr < GMLP_CHUNK else v[r0:r0 + tr, gs]).astype(BF16)
            o_ref[r0:r0 + tr, gs] = (u[r0:r0 + tr, gs] * (_dot(w, vg) + bias)).astype(o_ref.dtype)


def gmlp(za, zb, ln_g, ln_b, ws, bs, tr, act_dtype):
    t = zb.shape[0]
    tm = _tile(t, 4 * tr) if tr == GMLP_CHUNK else tr
    zspec = lambda off: pl.BlockSpec((tm, WIDTH), lambda i: (i, off // WIDTH))
    ospec = pl.BlockSpec((tm, WIDTH), lambda i: (i, 0))
    vec = pl.BlockSpec((1, WIDTH), lambda i: (0, 0))
    return pl.pallas_call(
        functools.partial(_gmlp_body, tr=tr),
        grid=(t // tm,),
        in_specs=[zspec(_ZA_CU), zspec(_ZB_CV), vec, vec,
                  pl.BlockSpec((HEADS, GMLP_CHUNK, GMLP_CHUNK), lambda i: (0, 0, 0)),
                  pl.BlockSpec((GMLP_CHUNK, HEADS), lambda i: (0, 0))],
        out_specs=[ospec, ospec],
        out_shape=[jax.ShapeDtypeStruct((t, WIDTH), act_dtype), jax.ShapeDtypeStruct((t, WIDTH), F32)],
        scratch_shapes=[pltpu.VMEM((GMLP_CHUNK, WIDTH), F32)],
        compiler_params=_params("parallel"),
        name="gmlp",
    )(za, zb, ln_g.reshape(1, WIDTH), ln_b.reshape(1, WIDTH), ws, bs.T)


def _merge_body(x_ref, gate_ref, oa_ref, ob_ref, oc_ref, wa_ref, wb_ref, wc_ref, wo_ref, o_ref):
    d = D_MODEL
    gate = lambda i: _sigmoid(gate_ref[:, i * d:(i + 1) * d].astype(F32))
    merged = (gate(0) * _dot(oa_ref[...].astype(BF16), wa_ref[...])
              + gate(1) * _dot(ob_ref[...].astype(BF16), wb_ref[...])
              + gate(2) * _dot(oc_ref[...].astype(BF16), wc_ref[...]))
    o_ref[...] = x_ref[...] + _dot(merged.astype(BF16), wo_ref[...])


def merge(x, z, oa, ob, oc, wa, wb, wc, wo, tm):
    t, d = x.shape
    row = lambda w: pl.BlockSpec((tm, w), lambda i: (i, 0))
    full = lambda a: pl.BlockSpec(a.shape, lambda i: (0, 0))
    return pl.pallas_call(
        _merge_body,
        grid=(t // tm,),
        in_specs=[row(d), row(3 * d), row(WIDTH), row(WIDTH), row(WIDTH),
                  full(wa), full(wb), full(wc), full(wo)],
        out_specs=row(d),
        out_shape=jax.ShapeDtypeStruct((t, d), F32),
        compiler_params=_params("parallel"),
        name="merge",
    )(x, z, oa, ob, oc, wa, wb, wc, wo)


def _xattn_body(x_ref, g_ref, wq_ref, mk_ref, mv_ref, wo_ref, o_ref, *, interleaved):
    x = x_ref[...]
    q = _dot(_rms(x, g_ref[...]).astype(BF16), wq_ref[...])
    if interleaved:
        m = mk_ref.shape[0] // HEADS
        head = lambda ref, h: ref[pl.ds(h, m, stride=HEADS), :]
    else:
        head = lambda ref, h: ref[:, h * HW:(h + 1) * HW]
    ss = [_dot_nt(q[:, h * HW:(h + 1) * HW].astype(BF16), head(mk_ref, h).astype(BF16)) * (HW ** -0.5)
          for h in range(HEADS)]
    es = [jnp.exp(s - jnp.max(s, axis=-1, keepdims=True)) for s in ss]
    ps = [(e * (1.0 / jnp.sum(e, axis=-1, keepdims=True))).astype(BF16) for e in es]
    o = jnp.concatenate([_dot(p, head(mv_ref, h).astype(BF16)) for h, p in enumerate(ps)], axis=-1)
    o_ref[...] = x + _dot(o.astype(BF16), wo_ref[...])


def xattn(x, g, wq, mem_k, mem_v, layer, wo, tm, tiles_per_seq):
    t, d = x.shape
    interleaved = mem_k.ndim == 5
    if interleaved:
        mem_k = mem_k.reshape(mem_k.shape[:2] + (mem_k.shape[2] * HEADS, HW))
        mem_v = mem_v.reshape(mem_k.shape)
    mem_spec = pl.BlockSpec((None, None) + mem_k.shape[2:], lambda i: (layer, i // tiles_per_seq, 0, 0))
    return pl.pallas_call(
        functools.partial(_xattn_body, interleaved=interleaved),
        grid=(t // tm,),
        in_specs=[pl.BlockSpec((tm, d), lambda i: (i, 0)),
                  pl.BlockSpec((1, d), lambda i: (0, 0)),
                  pl.BlockSpec(wq.shape, lambda i: (0, 0)),
                  mem_spec, mem_spec,
                  pl.BlockSpec(wo.shape, lambda i: (0, 0))],
        out_specs=pl.BlockSpec((tm, d), lambda i: (i, 0)),
        out_shape=jax.ShapeDtypeStruct((t, d), F32),
        compiler_params=_params("parallel"),
        name="xattn",
    )(x, g.reshape(1, d), wq, mem_k, mem_v, wo)


def _ffn_body(x_ref, g_ref, w1_ref, w2_ref, o_ref, *, tf):
    x = x_ref[...]
    h = _rms(x, g_ref[...]).astype(BF16)
    o_ref[...] = x
    nf = w1_ref.shape[1] // tf
    up = lambda j: _dot(h, w1_ref[:, j * tf:(j + 1) * tf])
    nxt = up(0)
    for j in range(nf):
        a = jnp.maximum(nxt, 0.0)
        if j + 1 < nf:
            nxt = up(j + 1)
        o_ref[...] += _dot((a * a).astype(BF16), w2_ref[j * tf:(j + 1) * tf, :])


def ffn(x, g, w1, w2, tm, tf):
    t, d = x.shape
    resident = lambda a: pl.BlockSpec(a.shape, lambda i: (0, 0), pipeline_mode=pl.Buffered(1))
    return pl.pallas_call(
        functools.partial(_ffn_body, tf=tf),
        grid=(t // tm,),
        in_specs=[pl.BlockSpec((tm, d), lambda i: (i, 0)),
                  pl.BlockSpec((1, d), lambda i: (0, 0)),
                  resident(w1), resident(w2)],
        out_specs=pl.BlockSpec((tm, d), lambda i: (i, 0)),
        out_shape=jax.ShapeDtypeStruct((t, d), F32),
        compiler_params=_params("parallel"),
        name="ffn",
    )(x, g.reshape(1, d), w1, w2)


def _final_norm_body(x_ref, g_ref, o_ref):
    o_ref[...] = _rms(x_ref[...], g_ref[...])


def final_norm(x, g, tm):
    t, d = x.shape
    return pl.pallas_call(
        _final_norm_body,
        grid=(t // tm,),
        in_specs=[pl.BlockSpec((tm, d), lambda i: (i, 0)), pl.BlockSpec((1, d), lambda i: (0, 0))],
        out_specs=pl.BlockSpec((tm, d), lambda i: (i, 0)),
        out_shape=jax.ShapeDtypeStruct((t, d), F32),
        compiler_params=_params("parallel"),
        name="final_norm",
    )(x, g.reshape(1, d))


def _regroup_w_in(w):
    o = 0
    parts = {}
    for name, size in (("q", WIDTH), ("k", WIDTH), ("v", WIDTH), ("a", HEADS), ("b", HEADS), ("g", WIDTH),
                       ("dq", WIDTH), ("dk", WIDTH), ("dv", WIDTH), ("cu", WIDTH), ("cv", WIDTH),
                       ("gates", 3 * D_MODEL)):
        parts[name] = w[..., o:o + size].astype(BF16)
        o += size
    used = _Z_HALF + _ZB_AB + 2 * HEADS
    cols = [parts[n] for n in ("gates", "g", "cu", "q", "k", "v", "dq", "dk", "dv", "cv", "a", "b")]
    cols.append(jnp.zeros(w.shape[:-1] + (2 * _Z_HALF - used,), BF16))
    return jnp.concatenate(cols, axis=-1)


def _tile(n, want):
    while n % want:
        want //= 2
    return want


def kernel(x_prompt, x_sample, cache_diff_k, cache_diff_v, page_table, cache_mem_k, cache_mem_v, state_gdn, state_gdn_conv, mem_prompt, w_in, norm_mix, gdn_conv_w, gdn_A_log, gdn_dt_bias, gdn_o_norm, diff_lq1, diff_lk1, diff_lq2, diff_lk2, diff_subln, gmlp_ln_g, gmlp_ln_b, gmlp_ws, gmlp_b, w_br_gdn, w_br_diff, w_br_gmlp, w_out, norm_xa, norm_mem, xa_wq, xa_wk, xa_wv, xa_wo, norm_ffn, ffn_w1, ffn_w2, norm_final):
    depth = w_in.shape[0]
    bp, lp, d = x_prompt.shape
    bs, ls, _ = x_sample.shape
    mem_len = mem_prompt.shape[1]
    past_len = page_table.shape[1] * PAGE
    lpad = SAMPLE_PAD
    bf = lambda a: a.astype(BF16)

    mem_tok = mem_prompt.reshape(bp * mem_len, d)
    mem_kv = [norm_matmul(mem_tok, norm_mem[l], bf(jnp.concatenate([xa_wk[l], xa_wv[l]], axis=1)),
                          _tile(bp * mem_len, 512), 512) for l in range(depth)]
    mem_k_p = jnp.stack([kv[:, :WIDTH] for kv in mem_kv]).reshape(depth, bp, mem_len, WIDTH)
    mem_v_p = jnp.stack([kv[:, WIDTH:] for kv in mem_kv]).reshape(depth, bp, mem_len, WIDTH)

    w_in_all = _regroup_w_in(w_in)
    weights = [dict(wa=bf(w_br_gdn[l]), wb=bf(w_br_diff[l]), wc=bf(w_br_gmlp[l]),
                    wo=bf(w_out[l]), xq=bf(xa_wq[l]), xo=bf(xa_wo[l]), w1=bf(ffn_w1[l]), w2=bf(ffn_w2[l]))
               for l in range(depth)]

    def trunk(x, b, l, l_valid, pos, mem_k, mem_v, s_in, buf_in, paged):
        t = b * l
        prompt = paged is None
        act = BF16 if prompt else F32
        tm = _tile(t, 1024)
        tables = _rope_tables(pos)
        outs = dict(s=[], buf=[], gv=[])
        k_all = v_all = None
        for li in range(depth):
            w = weights[li]
            lam_init = 0.8 - 0.6 * math.exp(-0.3 * li)
            lparams = (diff_lq1[li], diff_lk1[li], diff_lq2[li], diff_lk2[li])
            za, zb = proj_in(x, norm_mix[li], w_in_all, li, _tile(t, 512), 1024, act)
            za3, zb3 = za.reshape(b, l, _Z_HALF), zb.reshape(b, l, _Z_HALF)
            o_a, s_new = gdn(za3, zb3, buf_in[li], s_in, li, gdn_conv_w[li], gdn_A_log[li], gdn_dt_bias[li],
                             gdn_o_norm[li], _tile(l, 256) if prompt else GDN_CHUNK, _tile(l, 256) if prompt else l,
                             l_valid, act)
            outs["buf"].append(zb3[:, l_valid - (GDN_CONV - 1):l_valid, _ZB_QKV:_ZB_QKV + GDN_CONV_CH])
            qm, k_all, kb, v_all, vb = rope_split(zb, tables, _tile(l, 512), act, li, depth, k_all, v_all)
            if prompt:
                o_b = flash_diff(qm, kb, vb, lparams, diff_subln[li], lam_init, b, _tile(l, 2048), _tile(l, 1024))
            else:
                ck, cv, pt = paged
                o_b = paged_diff(qm.reshape(2, b, l, WIDTH), kb.reshape(b, l, WIDTH), vb.reshape(b, l, WIDTH),
                                 ck, cv, pt, li, lparams, diff_subln[li], lam_init, l_valid)
                o_b = o_b.reshape(t, WIDTH)
            o_c, gv = gmlp(za, zb, gmlp_ln_g[li], gmlp_ln_b[li], gmlp_ws[li], gmlp_b[li], min(l, GMLP_CHUNK), act)
            x = merge(x, za, o_a.reshape(t, WIDTH), o_b, o_c, w["wa"], w["wb"], w["wc"], w["wo"], _tile(t, 512))
            xa_tm = _tile(l, 1024)
            x = xattn(x, norm_xa[li], w["xq"], mem_k, mem_v, li, w["xo"], xa_tm, l // xa_tm)
            x = ffn(x, norm_ffn[li], w["w1"], w["w2"], tm, 1024)
            outs["s"].append(s_new)
            outs["gv"].append(gv)
        y = final_norm(x, norm_final, tm)
        seq = lambda a: a.reshape((depth, b, l) + a.shape[2:])[:, :, :l_valid]
        return (y.reshape(b, l, d)[:, :l_valid], seq(k_all), seq(v_all),
                jnp.stack(outs["s"]), jnp.stack(outs["buf"]), seq(jnp.stack(outs["gv"])))

    s0_p = jnp.zeros((depth, bp, HEADS, HW, HW), F32)
    buf0_p = jnp.zeros((depth, bp, GDN_CONV - 1, GDN_CONV_CH), F32)
    (y_p, k_p, v_p, s_p, buf_p, _) = trunk(x_prompt.reshape(bp * lp, d), bp, lp, lp, jnp.arange(lp),
                                           mem_k_p, mem_v_p, s0_p, buf0_p, None)

    x_s = jnp.pad(x_sample, ((0, 0), (0, lpad - ls), (0, 0))).reshape(bs * lpad, d)
    (y_s, k_s, v_s, s_s, buf_s, gv_s) = trunk(x_s, bs, lpad, ls, past_len + jnp.arange(lpad),
                                              cache_mem_k, cache_mem_v, state_gdn, state_gdn_conv,
                                              (cache_diff_k, cache_diff_v, page_table))

    return (y_p, y_s, k_p, v_p,
            mem_k_p.reshape(depth, bp, mem_len, HEADS, HW), mem_v_p.reshape(depth, bp, mem_len, HEADS, HW),
            s_p, buf_p, k_s, v_s, s_s, buf_s, gv_s)
```

```python
import functools
import math

import jax
import jax.numpy as jnp
from jax import lax
from jax.experimental import pallas as pl
from jax.experimental.pallas import tpu as pltpu

F32 = jnp.float32
BF16 = jnp.bfloat16
EPS = 1e-6
NEG = -1e30

D_MODEL = 1024
HEADS = 4
HW = 128
WIDTH = HEADS * HW
DIFF_DH = 64
ROPE_DIM = 16
ROPE_THETA = 500000.0
GDN_CONV = 4
GDN_CHUNK = 64
GDN_CONV_CH = 3 * WIDTH
GMLP_CHUNK = 128
PAGE = 128
SAMPLE_PAD = 8

_ZA_GATES = 0
_ZA_GIN = 3 * D_MODEL
_ZA_CU = _ZA_GIN + WIDTH
_ZB_QKV = 0
_ZB_DQ = GDN_CONV_CH
_ZB_DK = _ZB_DQ + WIDTH
_ZB_DV = _ZB_DK + WIDTH
_ZB_CV = _ZB_DV + WIDTH
_ZB_AB = _ZB_CV + WIDTH
_Z_HALF = 4096

_VMEM_LIMIT = 56 * 1024 * 1024


def _params(*sem):
    return pltpu.CompilerParams(dimension_semantics=sem, vmem_limit_bytes=_VMEM_LIMIT)


def _rms(x, g):
    return x * lax.rsqrt(jnp.mean(x * x, axis=-1, keepdims=True) + EPS) * g


def _sigmoid(x):
    return 1.0 / (1.0 + jnp.exp(-x))


def _dot(a, b):
    return jnp.dot(a, b, preferred_element_type=F32)


def _dot_nt(a, b):
    return lax.dot_general(a, b, (((1,), (1,)), ((), ())), preferred_element_type=F32)


def _dot_tn(a, b):
    return lax.dot_general(a, b, (((0,), (0,)), ((), ())), preferred_element_type=F32)


def _split(a):
    hi = lax.bitcast_convert_type(lax.bitcast_convert_type(a, jnp.int32) & jnp.int32(-65536), F32)
    return hi, a - hi


def _hi_lhs(a):
    hi, lo = _split(a)
    return jnp.concatenate([hi, hi, lo], axis=1).astype(BF16)


def _hi_rhs(b):
    hi, lo = _split(b)
    return jnp.concatenate([hi, lo, hi], axis=0).astype(BF16)


def _dot_hi(a, b):
    return _dot(_hi_lhs(a), _hi_rhs(b))


def _dot_hi_exact_lhs(a_bf16, b):
    hi, lo = _split(b)
    lo_hi, lo_lo = _split(lo)
    return _dot(jnp.concatenate([a_bf16] * 3, axis=1), jnp.concatenate([hi, lo_hi, lo_lo], axis=0).astype(BF16))


def _dot_hi_exact_rhs(a, b_bf16):
    hi, lo = _split(a)
    lo_hi, lo_lo = _split(lo)
    return _dot(jnp.concatenate([hi, lo_hi, lo_lo], axis=1).astype(BF16), jnp.concatenate([b_bf16] * 3, axis=0))


def _norm_mm_body(x_ref, g_ref, w_ref, o_ref, h_ref):
    @pl.when(pl.program_id(1) == 0)
    def _():
        h_ref[...] = _rms(x_ref[...], g_ref[...]).astype(BF16)

    o_ref[...] = _dot(h_ref[...], w_ref[...]).astype(o_ref.dtype)


def norm_matmul(x, g, w, tm, tn):
    t, d = x.shape
    n = w.shape[1]
    return pl.pallas_call(
        _norm_mm_body,
        grid=(t // tm, n // tn),
        in_specs=[pl.BlockSpec((tm, d), lambda i, j: (i, 0)),
                  pl.BlockSpec((1, d), lambda i, j: (0, 0)),
                  pl.BlockSpec((d, tn), lambda i, j: (0, j))],
        out_specs=pl.BlockSpec((tm, tn), lambda i, j: (i, j)),
        out_shape=jax.ShapeDtypeStruct((t, n), F32),
        scratch_shapes=[pltpu.VMEM((tm, d), BF16)],
        compiler_params=_params("parallel", "arbitrary"),
        name="norm_matmul",
    )(x, g.reshape(1, d), w)


def _proj_in_body(x_ref, g_ref, w_ref, oa_ref, ob_ref, *, tn):
    h = _rms(x_ref[...], g_ref[...]).astype(BF16)
    for j in range(_Z_HALF // tn):
        oa_ref[:, j * tn:(j + 1) * tn] = _dot(h, w_ref[:, j * tn:(j + 1) * tn]).astype(oa_ref.dtype)
    for j in range(_Z_HALF // tn):
        ob_ref[:, j * tn:(j + 1) * tn] = _dot(h, w_ref[:, _Z_HALF + j * tn:_Z_HALF + (j + 1) * tn])


def proj_in(x, g, w_all, layer, tm, tn, act_dtype):
    t, d = x.shape
    return pl.pallas_call(
        functools.partial(_proj_in_body, tn=tn),
        grid=(t // tm,),
        in_specs=[pl.BlockSpec((tm, d), lambda i: (i, 0)),
                  pl.BlockSpec((1, d), lambda i: (0, 0)),
                  pl.BlockSpec((None, d, 2 * _Z_HALF), lambda i: (layer, 0, 0), pipeline_mode=pl.Buffered(1))],
        out_specs=[pl.BlockSpec((tm, _Z_HALF), lambda i: (i, 0)),
                   pl.BlockSpec((tm, _Z_HALF), lambda i: (i, 0))],
        out_shape=[jax.ShapeDtypeStruct((t, _Z_HALF), act_dtype), jax.ShapeDtypeStruct((t, _Z_HALF), F32)],
        compiler_params=_params("parallel"),
        name="proj_in",
    )(x, g.reshape(1, d), w_all)


def _gdn_body(qkv_ref, gin_ref, ab_ref, buf_ref, s0_ref, cw_ref, alog_ref, dtb_ref, onorm_ref,
              o_ref, sout_ref, xs_ref, ab_scr, s_ref, *, tl, tl_in, valid):
    c = GDN_CHUNK
    nc = tl // c
    nb = qkv_ref.shape[0]
    t = pl.program_id(1)

    @pl.when(t == 0)
    def _():
        s_ref[...] = s0_ref[...]
        xs_ref[...] = jnp.zeros_like(xs_ref)
        xs_ref[:, 5:8, :] = buf_ref[...]
        ab_scr[...] = jnp.zeros_like(ab_scr)

    ti = lax.broadcasted_iota(jnp.int32, (tl, tl), 0)
    tj = lax.broadcasted_iota(jnp.int32, (tl, tl), 1)
    same_chunk = (ti // c) == (tj // c)
    sum_rows = jnp.where(same_chunk & (ti >= tj), 1.0, 0.0).astype(BF16)
    sum_lanes = jnp.where(same_chunk & (ti <= tj), 1.0, 0.0).astype(BF16)
    w = cw_ref[...]

    def front(bi):
        xs_ref[bi, 8:8 + tl_in, :] = qkv_ref[bi]
        ab_scr[bi, 0:tl_in, :] = ab_ref[bi]
        xs = xs_ref[bi]
        y = xs[8:] * w[3:4]
        for i in range(1, GDN_CONV):
            y = y + pltpu.roll(xs, i, 0)[8:] * w[3 - i:4 - i]
        xs_ref[bi, 5:8, :] = xs_ref[bi, tl + 5:tl + 8, :]
        y = y * _sigmoid(y)
        ab = ab_scr[bi]
        sp_in = ab + dtb_ref[...]
        softplus = jnp.maximum(sp_in, 0.0) + jnp.log1p(jnp.exp(-jnp.abs(sp_in)))
        g = -jnp.exp(alog_ref[...]) * softplus
        beta = _sigmoid(ab)
        if valid < tl:
            live = lax.broadcasted_iota(jnp.int32, (tl, 1), 0) < valid
            g = jnp.where(live, g, 0.0)
            beta = jnp.where(live, beta, 0.0)
        gcol = _dot_hi_exact_lhs(sum_rows, g)
        grow = _dot_hi_exact_rhs(g.T[0:8, :], sum_lanes)
        return y, beta, gcol, grow

    fronts = [front(bi) for bi in range(nb)]

    ii = lax.broadcasted_iota(jnp.int32, (c, c), 0)
    jj = lax.broadcasted_iota(jnp.int32, (c, c), 1)
    incl = ii >= jj
    strict = ii > jj
    eye = (ii == jj).astype(F32)

    def prepare(bi, ci, h):
        y, beta, gcol, grow = fronts[bi]
        rs = slice(ci * c, (ci + 1) * c)
        gi = gcol[rs, h:h + 1]
        gj = grow[h:h + 1, rs]
        dec_incl = jnp.exp(jnp.where(incl, gi - gj, NEG))
        glast = gcol[(ci + 1) * c - 1:(ci + 1) * c, h:h + 1]
        qh = y[rs, h * HW:(h + 1) * HW]
        kh = y[rs, WIDTH + h * HW:WIDTH + (h + 1) * HW]
        qh = qh * lax.rsqrt(jnp.sum(qh * qh, axis=-1, keepdims=True) + EPS) * (HW ** -0.5)
        kh = kh * lax.rsqrt(jnp.sum(kh * kh, axis=-1, keepdims=True) + EPS)
        return dict(dec_incl=dec_incl, dec_strict=jnp.where(strict, dec_incl, 0.0), gexp=jnp.exp(gi),
                    bcol=beta[rs, 4 + h:5 + h], kh=kh, vh=y[rs, 2 * WIDTH + h * HW:2 * WIDTH + (h + 1) * HW],
                    qb=qh.astype(BF16), kb=kh.astype(BF16), tail=jnp.exp(glast - gi), gend=jnp.exp(glast))

    terms = [prepare(bi, ci, h) for ci in range(nc) for bi in range(nb) for h in range(HEADS)]
    for m in terms:
        kk = _dot_nt(m["kb"], m["kb"])
        m["p"] = (_dot_nt(m["qb"], m["kb"]) * m["dec_incl"]).astype(BF16)
        m["x"] = -(m["bcol"] * kk * m["dec_strict"])
        m["tinv"] = eye + m["x"]
    for m in terms:
        m["x"] = _dot_hi(m["x"], m["x"])
    for _ in range(int(math.log2(c)) - 2):
        for m in terms:
            r = _dot_hi(jnp.concatenate([m["x"], m["tinv"]], axis=0), m["x"])
            m["x"] = r[0:c]
            m["tinv"] = m["tinv"] + r[c:2 * c]
    for m in terms:
        m["tinv"] = m["tinv"] + _dot_hi(m["tinv"], m["x"])
    for m in terms:
        sol = _dot_hi(m["tinv"], jnp.concatenate([m["bcol"] * m["vh"], (m["bcol"] * m["gexp"]) * m["kh"]], axis=-1))
        m["u0"] = sol[:, :HW]
        m["w"] = sol[:, HW:].astype(BF16)

    seq_heads = [(bi, h) for bi in range(nb) for h in range(HEADS)]
    state = [s_ref[bi, h] for bi, h in seq_heads]
    for ci in range(nc):
        ms = terms[ci * len(seq_heads):(ci + 1) * len(seq_heads)]
        sbs = [s.astype(BF16) for s in state]
        us = [m["u0"] - _dot_nt(m["w"], sb) for m, sb in zip(ms, sbs)]
        qs = [_dot_nt(m["qb"], sb) for m, sb in zip(ms, sbs)]
        pus = [_dot(m["p"], u.astype(BF16)) for m, u in zip(ms, us)]
        state = [m["gend"] * s + _dot_tn((u * m["tail"]).astype(BF16), m["kb"])
                 for m, s, u in zip(ms, state, us)]
        r0 = ci * c
        rows = min(c, tl_in - r0)
        if rows > 0:
            for (bi, h), m, qs_h, pu in zip(seq_heads, ms, qs, pus):
                hs = slice(h * HW, (h + 1) * HW)
                o = _rms(m["gexp"] * qs_h + pu, onorm_ref[...])[0:rows]
                gin = gin_ref[bi, r0:r0 + rows, hs].astype(F32)
                o_ref[bi, r0:r0 + rows, hs] = (o * (gin * _sigmoid(gin))).astype(o_ref.dtype)
    for (bi, h), s in zip(seq_heads, state):
        s_ref[bi, h] = s

    @pl.when(t == pl.num_programs(1) - 1)
    def _():
        sout_ref[...] = s_ref[...]


def gdn(za3, zb3, conv_buf, s0, layer, conv_w, a_log, dt_bias, o_norm, tl, tl_in, valid, act_dtype):
    b, l, _ = zb3.shape
    nt = l // tl_in
    nb = 4 if (nt == 1 and b % 4 == 0) else 1
    assert tl % GDN_CHUNK == 0 and (tl_in == tl or nt == 1)
    lane_pad = lambda v, off: jnp.zeros((1, HW), F32).at[0, off:off + HEADS].set(v)
    body = functools.partial(_gdn_body, tl=tl, tl_in=tl_in, valid=valid)
    return pl.pallas_call(
        body,
        grid=(b // nb, nt),
        in_specs=[pl.BlockSpec((nb, tl_in, GDN_CONV_CH), lambda i, t: (i, t, _ZB_QKV // GDN_CONV_CH)),
                  pl.BlockSpec((nb, tl_in, WIDTH), lambda i, t: (i, t, _ZA_GIN // WIDTH)),
                  pl.BlockSpec((nb, tl_in, HW), lambda i, t: (i, t, _ZB_AB // HW)),
                  pl.BlockSpec((nb, GDN_CONV - 1, GDN_CONV_CH), lambda i, t: (i, 0, 0)),
                  pl.BlockSpec((None, nb, HEADS, HW, HW), lambda i, t: (layer, i, 0, 0, 0)),
                  pl.BlockSpec((GDN_CONV, GDN_CONV_CH), lambda i, t: (0, 0)),
                  pl.BlockSpec((1, HW), lambda i, t: (0, 0)),
                  pl.BlockSpec((1, HW), lambda i, t: (0, 0)),
                  pl.BlockSpec((1, HW), lambda i, t: (0, 0))],
        out_specs=[pl.BlockSpec((nb, tl_in, WIDTH), lambda i, t: (i, t, 0)),
                   pl.BlockSpec((nb, HEADS, HW, HW), lambda i, t: (i, 0, 0, 0))],
        out_shape=[jax.ShapeDtypeStruct((b, l, WIDTH), act_dtype),
                   jax.ShapeDtypeStruct((b, HEADS, HW, HW), F32)],
        scratch_shapes=[pltpu.VMEM((nb, tl + 8, GDN_CONV_CH), F32),
                        pltpu.VMEM((nb, tl, HW), F32),
                        pltpu.VMEM((nb, HEADS, HW, HW), F32)],
        compiler_params=_params("parallel", "arbitrary"),
        name="gdn",
    )(zb3, za3, zb3, conv_buf, s0, conv_w, lane_pad(a_log, 0), lane_pad(dt_bias, 0), o_norm.reshape(1, HW))


def _rope_tables(pos):
    half = ROPE_DIM // 2
    inv = ROPE_THETA ** (-jnp.arange(0, ROPE_DIM, 2, dtype=F32) / ROPE_DIM)
    ang = pos.astype(F32)[:, None] * inv[None, :]
    m = jnp.arange(HW) % DIFF_DH
    cos = jnp.cos(ang)[:, m % half]
    sin = jnp.sin(ang)[:, m % half]
    cos_t = jnp.where(m < ROPE_DIM, cos, 1.0)
    sin_a = jnp.where(m < half, -sin, 0.0)
    sin_b = jnp.where((m >= half) & (m < ROPE_DIM), sin, 0.0)
    return cos_t.astype(F32), sin_a.astype(F32), sin_b.astype(F32)


def _rope_body(q_ref, k_ref, v_ref, cos_ref, sa_ref, sb_ref, *refs):
    qm_ref, kr_ref, kb_ref, vf_ref, vb_ref = refs[-5:]
    half = ROPE_DIM // 2
    cos, sa, sb = cos_ref[...], sa_ref[...], sb_ref[...]
    lane = lax.broadcasted_iota(jnp.int32, (1, HW), 1)
    first_map = lane < DIFF_DH

    def rot(x):
        return x * cos + pltpu.roll(x, HW - half, 1) * sa + pltpu.roll(x, half, 1) * sb

    for h in range(HEADS):
        hs = slice(h * HW, (h + 1) * HW)
        q = rot(q_ref[:, hs]) * (DIFF_DH ** -0.5)
        qm_ref[0, :, hs] = jnp.where(first_map, q, 0.0).astype(qm_ref.dtype)
        qm_ref[1, :, hs] = jnp.where(first_map, 0.0, q).astype(qm_ref.dtype)
        k = rot(k_ref[:, hs])
        kr_ref[:, h, :] = k
        kb_ref[:, hs] = k.astype(kb_ref.dtype)
        vf_ref[:, h, :] = v_ref[:, hs]
    vb_ref[...] = v_ref[...].astype(vb_ref.dtype)


def rope_split(z, tables, tr, act_dtype, layer, depth, k_all, v_all):
    t = z.shape[0]
    nl = tables[0].shape[0] // tr
    zspec = lambda off: pl.BlockSpec((tr, WIDTH), lambda i: (i, off // WIDTH))
    tspec = pl.BlockSpec((tr, HW), lambda i: (i % nl, 0))
    ospec = pl.BlockSpec((tr, WIDTH), lambda i: (i, 0))
    hspec = pl.BlockSpec((None, tr, HEADS, HW), lambda i: (layer, i, 0, 0))
    carried = [] if k_all is None else [k_all, v_all]
    return pl.pallas_call(
        _rope_body,
        grid=(t // tr,),
        in_specs=[zspec(_ZB_DQ), zspec(_ZB_DK), zspec(_ZB_DV), tspec, tspec, tspec]
                 + [pl.BlockSpec(memory_space=pl.ANY)] * len(carried),
        out_specs=[pl.BlockSpec((2, tr, WIDTH), lambda i: (0, i, 0)), hspec, ospec, hspec, ospec],
        out_shape=[jax.ShapeDtypeStruct((2, t, WIDTH), act_dtype),
                   jax.ShapeDtypeStruct((depth, t, HEADS, HW), F32),
                   jax.ShapeDtypeStruct((t, WIDTH), act_dtype),
                   jax.ShapeDtypeStruct((depth, t, HEADS, HW), F32),
                   jax.ShapeDtypeStruct((t, WIDTH), act_dtype)],
        input_output_aliases={6: 1, 7: 3} if carried else {},
        compiler_params=_params("parallel"),
        name="rope_split",
    )(z, z, z, *tables, *carried)


def _lambda(lq1_ref, lk1_ref, lq2_ref, lk2_ref, lam_init):
    s1 = jnp.sum(lq1_ref[...] * lk1_ref[...], axis=-1, keepdims=True)
    s2 = jnp.sum(lq2_ref[...] * lk2_ref[...], axis=-1, keepdims=True)
    return jnp.exp(s1) - jnp.exp(s2) + lam_init


_FLASH_ROWS = 256


def _flash_body(qi_ref, ki_ref, q_ref, k_ref, v_ref, lq1_ref, lk1_ref, lq2_ref, lk2_ref, sub_ref, o_ref,
                m_ref, acc_ref, *, lam_init):
    pair = pl.program_id(2)
    qi = qi_ref[pair]
    ki = ki_ref[pair]
    tq = q_ref.shape[1]
    tk = k_ref.shape[0]
    ratio = tq // tk
    rows = min(tq, _FLASH_ROWS)

    @pl.when(ki == 0)
    def _():
        m_ref[...] = jnp.full_like(m_ref, NEG)
        acc_ref[...] = jnp.zeros_like(acc_ref)

    def block(diag):
        k = k_ref[...]
        v = v_ref[...]
        v1 = jnp.concatenate([v, jnp.ones_like(v)], axis=1)
        col0 = 0 if diag is None else diag * tk
        chains = [(c, r0) for c in range(2) for r0 in range(0, tq, rows) if r0 + rows > col0]

        def n_keys(r0):
            return tk if diag is None else min(tk, r0 + rows - col0)

        def scores(c, r0):
            kc = n_keys(r0)
            s = _dot_nt(q_ref[c, r0:r0 + rows, :], k[0:kc])
            if diag is not None and r0 < col0 + kc - 1:
                ri = lax.broadcasted_iota(jnp.int32, s.shape, 0) + r0
                ci = lax.broadcasted_iota(jnp.int32, s.shape, 1) + col0
                s = jnp.where(ci <= ri, s, NEG)
            return s

        ahead = 3
        pending = [scores(*ch) for ch in chains[:ahead]]
        for i, (c, r0) in enumerate(chains):
            rs = slice(r0, r0 + rows)
            s = pending.pop(0)
            kc = n_keys(r0)
            tiles = [s[:, j * HW:(j + 1) * HW] for j in range(kc // HW)]
            tile_max = functools.reduce(jnp.maximum, tiles)
            m_prev = m_ref[c, rs, :]
            m_new = jnp.maximum(m_prev, jnp.max(tile_max, axis=-1, keepdims=True))
            alpha = jnp.exp(m_prev - m_new)
            p = jnp.concatenate([jnp.exp(tl - m_new).astype(BF16) for tl in tiles], axis=1)
            pv = _dot(p, v1[0:kc])
            if i + ahead < len(chains):
                pending.append(scores(*chains[i + ahead]))
            acc_ref[c, rs, :] = jnp.concatenate([alpha, alpha], axis=1) * acc_ref[c, rs, :] + pv
            m_ref[c, rs, :] = m_new

    pl.when(ki < qi * ratio)(functools.partial(block, None))
    for d in range(ratio):
        pl.when(ki == qi * ratio + d)(functools.partial(block, d))

    @pl.when(ki == qi * ratio + ratio - 1)
    def _():
        lam = _lambda(lq1_ref, lk1_ref, lq2_ref, lk2_ref, lam_init)
        o = (acc_ref[0, :, 0:HW] * (1.0 / acc_ref[0, :, HW:2 * HW])
             - lam * (acc_ref[1, :, 0:HW] * (1.0 / acc_ref[1, :, HW:2 * HW])))
        o_ref[...] = (_rms(o, sub_ref[...]) * (1.0 - lam_init)).astype(o_ref.dtype)


def flash_diff(qm, kb, vb, lparams, subln, lam_init, batch, tq, tk):
    t = kb.shape[0]
    l = t // batch
    nq = l // tq
    nk = l // tk
    assert tq % tk == 0
    pairs = [(i, j) for i in range(nq) for j in range((i + 1) * (tq // tk))]
    qi_arr = jnp.asarray([p[0] for p in pairs], jnp.int32)
    ki_arr = jnp.asarray([p[1] for p in pairs], jnp.int32)
    lspec = pl.BlockSpec((1, DIFF_DH), lambda b, h, p, qi, ki: (0, 0))
    kvspec = pl.BlockSpec((tk, HW), lambda b, h, p, qi, ki: (b * nk + ki[p], h))
    grid_spec = pltpu.PrefetchScalarGridSpec(
        num_scalar_prefetch=2,
        grid=(batch, HEADS, len(pairs)),
        in_specs=[pl.BlockSpec((2, tq, HW), lambda b, h, p, qi, ki: (0, b * nq + qi[p], h)),
                  kvspec, kvspec, lspec, lspec, lspec, lspec,
                  pl.BlockSpec((1, HW), lambda b, h, p, qi, ki: (0, 0))],
        out_specs=pl.BlockSpec((tq, HW), lambda b, h, p, qi, ki: (b * nq + qi[p], h)),
        scratch_shapes=[pltpu.VMEM((2, tq, HW), F32), pltpu.VMEM((2, tq, 2 * HW), F32)],
    )
    return pl.pallas_call(
        functools.partial(_flash_body, lam_init=lam_init),
        grid_spec=grid_spec,
        out_shape=jax.ShapeDtypeStruct((t, WIDTH), BF16),
        compiler_params=_params("parallel", "parallel", "arbitrary"),
        name="flash_diff",
    )(qi_arr, ki_arr, qm, kb, vb, *[p.reshape(1, DIFF_DH) for p in lparams], subln.reshape(1, HW))


_PAGES_PER_STEP = 32


def _paged_body(pt_ref, q_ref, *refs, lam_init, n_valid):
    del pt_ref
    pp = _PAGES_PER_STEP
    k_refs, v_refs = refs[:pp], refs[pp:2 * pp]
    (kn_ref, vn_ref, lq1_ref, lk1_ref, lq2_ref, lk2_ref, sub_ref, o_ref,
     m_ref, l_ref, acc_ref, kn_scr, vn_scr) = refs[2 * pp:]
    p_idx = pl.program_id(1)
    nq = q_ref.shape[1]

    @pl.when(p_idx == 0)
    def _():
        m_ref[...] = jnp.full_like(m_ref, NEG)
        l_ref[...] = jnp.zeros_like(l_ref)
        acc_ref[...] = jnp.zeros_like(acc_ref)

    def head_q(h):
        hs = slice(h * HW, (h + 1) * HW)
        return jnp.concatenate([q_ref[0, :, hs], q_ref[1, :, hs]], axis=0).astype(BF16)

    def update(k_page, v_page, n_pages, mask):
        qs = [head_q(h) for h in range(HEADS)]
        s = jnp.concatenate(
            [jnp.concatenate([_dot_nt(qs[h], k_page(j, h).astype(BF16)) for j in range(n_pages)], axis=-1)
             for h in range(HEADS)], axis=0)
        if mask is not None:
            s = jnp.where(mask, s, NEG)
        m_prev = m_ref[...]
        m_new = jnp.maximum(m_prev, jnp.max(s, axis=-1, keepdims=True))
        alpha = jnp.exp(m_prev - m_new)
        p = jnp.exp(s - m_new).astype(BF16)
        l_ref[...] = alpha * l_ref[...] + jnp.sum(p.astype(F32), axis=-1, keepdims=True)
        pvs = []
        for h in range(HEADS):
            ph = p[h * 2 * nq:(h + 1) * 2 * nq]
            pv = _dot(ph[:, 0:PAGE], v_page(0, h).astype(BF16))
            for j in range(1, n_pages):
                pv = pv + _dot(ph[:, j * PAGE:(j + 1) * PAGE], v_page(j, h).astype(BF16))
            pvs.append(pv)
        acc_ref[...] = alpha * acc_ref[...] + jnp.concatenate(pvs, axis=0)
        m_ref[...] = m_new

    head_rows = lambda h: pl.ds(h, PAGE, stride=HEADS)
    update(lambda j, h: k_refs[j][head_rows(h), :], lambda j, h: v_refs[j][head_rows(h), :], pp, None)

    @pl.when(p_idx == pl.num_programs(1) - 1)
    def _():
        kn_scr[...] = jnp.zeros_like(kn_scr)
        vn_scr[...] = jnp.zeros_like(vn_scr)
        kn_scr[0:nq, :] = kn_ref[...]
        vn_scr[0:nq, :] = vn_ref[...]
        ri = lax.broadcasted_iota(jnp.int32, (HEADS * 2 * nq, PAGE), 0) % nq
        ci = lax.broadcasted_iota(jnp.int32, (HEADS * 2 * nq, PAGE), 1)
        update(lambda j, h: kn_scr[:, h * HW:(h + 1) * HW], lambda j, h: vn_scr[:, h * HW:(h + 1) * HW], 1,
               (ci <= ri) & (ci < n_valid))
        lam = _lambda(lq1_ref, lk1_ref, lq2_ref, lk2_ref, lam_init)
        for h in range(HEADS):
            hs = slice(h * HW, (h + 1) * HW)
            r0 = slice(h * 2 * nq, h * 2 * nq + nq)
            r1 = slice(h * 2 * nq + nq, (h + 1) * 2 * nq)
            o = acc_ref[r0] * (1.0 / l_ref[r0]) - lam * (acc_ref[r1] * (1.0 / l_ref[r1]))
            o_ref[:, hs] = (_rms(o, sub_ref[...]) * (1.0 - lam_init)).astype(o_ref.dtype)


def paged_diff(qm, kb, vb, cache_k, cache_v, page_table, layer, lparams, subln, lam_init, n_valid):
    _, b, nq, _ = qm.shape
    n_pages = page_table.shape[1]
    pp = _PAGES_PER_STEP
    assert n_pages % pp == 0
    depth, n_pool = cache_k.shape[:2]
    cache_k = cache_k.reshape(depth, n_pool, PAGE * HEADS, HW)
    cache_v = cache_v.reshape(depth, n_pool, PAGE * HEADS, HW)
    page_spec = lambda j: pl.BlockSpec((None, None, PAGE * HEADS, HW),
                                       lambda i, p, pt: (layer, pt[i, p * pp + j], 0, 0))
    page_specs = [page_spec(j) for j in range(pp)]
    new_spec = pl.BlockSpec((None, nq, WIDTH), lambda i, p, pt: (i, 0, 0))
    lspec = pl.BlockSpec((1, DIFF_DH), lambda i, p, pt: (0, 0))
    grid_spec = pltpu.PrefetchScalarGridSpec(
        num_scalar_prefetch=1,
        grid=(b, n_pages // pp),
        in_specs=([pl.BlockSpec((2, None, nq, WIDTH), lambda i, p, pt: (0, i, 0, 0))]
                  + page_specs + page_specs
                  + [new_spec, new_spec, lspec, lspec, lspec, lspec,
                     pl.BlockSpec((1, HW), lambda i, p, pt: (0, 0))]),
        out_specs=pl.BlockSpec((None, nq, WIDTH), lambda i, p, pt: (i, 0, 0)),
        scratch_shapes=[pltpu.VMEM((HEADS * 2 * nq, 1), F32), pltpu.VMEM((HEADS * 2 * nq, 1), F32),
                        pltpu.VMEM((HEADS * 2 * nq, HW), F32),
                        pltpu.VMEM((PAGE, WIDTH), F32), pltpu.VMEM((PAGE, WIDTH), F32)],
    )
    return pl.pallas_call(
        functools.partial(_paged_body, lam_init=lam_init, n_valid=n_valid),
        grid_spec=grid_spec,
        out_shape=jax.ShapeDtypeStruct((b, nq, WIDTH), F32),
        compiler_params=_params("parallel", "arbitrary"),
        name="paged_diff",
    )(page_table, qm, *([cache_k] * pp), *([cache_v] * pp), kb, vb,
      *[p.reshape(1, DIFF_DH) for p in lparams], subln.reshape(1, HW))


def _gelu(x):
    return 0.5 * x * (1.0 + jnp.tanh(math.sqrt(2.0 / math.pi) * (x + 0.044715 * (x * x * x))))


def _gmlp_body(u_ref, v_ref, lg_ref, lb_ref, ws_ref, bst_ref, o_ref, gv_ref, v_scr, *, tr):
    u = _gelu(u_ref[...].astype(F32))
    v = _gelu(v_ref[...])
    mu = jnp.mean(v, axis=-1, keepdims=True)
    vc = v - mu
    v = vc * lax.rsqrt(jnp.mean(vc * vc, axis=-1, keepdims=True) + EPS) * lg_ref[...] + lb_ref[...]
    gv_ref[...] = v
    if tr < GMLP_CHUNK:
        v_scr[...] = jnp.zeros_like(v_scr)
        v_scr[0:tr, :] = v
    ii = lax.broadcasted_iota(jnp.int32, (tr, GMLP_CHUNK), 0)
    jj = lax.broadcasted_iota(jnp.int32, (tr, GMLP_CHUNK), 1)
    for g in range(HEADS):
        gs = slice(g * HW, (g + 1) * HW)
        w = jnp.where(ii >= jj, ws_ref[g, 0:tr, :], 0.0).astype(BF16)
        bias = bst_ref[0:tr, g:g + 1]
        for r0 in range(0, u.shape[0], tr):
            vg = (v_scr[:, gs] if tr < GMLP_CHUNK else v[r0:r0 + tr, gs]).astype(BF16)
            o_ref[r0:r0 + tr, gs] = (u[r0:r0 + tr, gs] * (_dot(w, vg) + bias)).astype(o_ref.dtype)


def gmlp(za, zb, ln_g, ln_b, ws, bs, tr, act_dtype):
    t = zb.shape[0]
    tm = _tile(t, 4 * tr) if tr == GMLP_CHUNK else tr
    zspec = lambda off: pl.BlockSpec((tm, WIDTH), lambda i: (i, off // WIDTH))
    ospec = pl.BlockSpec((tm, WIDTH), lambda i: (i, 0))
    vec = pl.BlockSpec((1, WIDTH), lambda i: (0, 0))
    return pl.pallas_call(
        functools.partial(_gmlp_body, tr=tr),
        grid=(t // tm,),
        in_specs=[zspec(_ZA_CU), zspec(_ZB_CV), vec, vec,
                  pl.BlockSpec((HEADS, GMLP_CHUNK, GMLP_CHUNK), lambda i: (0, 0, 0)),
                  pl.BlockSpec((GMLP_CHUNK, HEADS), lambda i: (0, 0))],
        out_specs=[ospec, ospec],
        out_shape=[jax.ShapeDtypeStruct((t, WIDTH), act_dtype), jax.ShapeDtypeStruct((t, WIDTH), F32)],
        scratch_shapes=[pltpu.VMEM((GMLP_CHUNK, WIDTH), F32)],
        compiler_params=_params("parallel"),
        name="gmlp",
    )(za, zb, ln_g.reshape(1, WIDTH), ln_b.reshape(1, WIDTH), ws, bs.T)


def _merge_body(x_ref, gate_ref, oa_ref, ob_ref, oc_ref, wa_ref, wb_ref, wc_ref, wo_ref, o_ref):
    d = D_MODEL
    gate = lambda i: _sigmoid(gate_ref[:, i * d:(i + 1) * d].astype(F32))
    merged = (gate(0) * _dot(oa_ref[...].astype(BF16), wa_ref[...])
              + gate(1) * _dot(ob_ref[...].astype(BF16), wb_ref[...])
              + gate(2) * _dot(oc_ref[...].astype(BF16), wc_ref[...]))
    o_ref[...] = x_ref[...] + _dot(merged.astype(BF16), wo_ref[...])


def merge(x, z, oa, ob, oc, wa, wb, wc, wo, tm):
    t, d = x.shape
    row = lambda w: pl.BlockSpec((tm, w), lambda i: (i, 0))
    full = lambda a: pl.BlockSpec(a.shape, lambda i: (0, 0))
    return pl.pallas_call(
        _merge_body,
        grid=(t // tm,),
        in_specs=[row(d), row(3 * d), row(WIDTH), row(WIDTH), row(WIDTH),
                  full(wa), full(wb), full(wc), full(wo)],
        out_specs=row(d),
        out_shape=jax.ShapeDtypeStruct((t, d), F32),
        compiler_params=_params("parallel"),
        name="merge",
    )(x, z, oa, ob, oc, wa, wb, wc, wo)


def _xattn_body(x_ref, g_ref, wq_ref, mk_ref, mv_ref, wo_ref, o_ref, *, interleaved):
    x = x_ref[...]
    q = _dot(_rms(x, g_ref[...]).astype(BF16), wq_ref[...])
    if interleaved:
        m = mk_ref.shape[0] // HEADS
        head = lambda ref, h: ref[pl.ds(h, m, stride=HEADS), :]
    else:
        head = lambda ref, h: ref[:, h * HW:(h + 1) * HW]
    ss = [_dot_nt(q[:, h * HW:(h + 1) * HW].astype(BF16), head(mk_ref, h).astype(BF16)) * (HW ** -0.5)
          for h in range(HEADS)]
    es = [jnp.exp(s - jnp.max(s, axis=-1, keepdims=True)) for s in ss]
    ps = [(e * (1.0 / jnp.sum(e, axis=-1, keepdims=True))).astype(BF16) for e in es]
    o = jnp.concatenate([_dot(p, head(mv_ref, h).astype(BF16)) for h, p in enumerate(ps)], axis=-1)
    o_ref[...] = x + _dot(o.astype(BF16), wo_ref[...])


def xattn(x, g, wq, mem_k, mem_v, layer, wo, tm, tiles_per_seq):
    t, d = x.shape
    interleaved = mem_k.ndim == 5
    if interleaved:
        mem_k = mem_k.reshape(mem_k.shape[:2] + (mem_k.shape[2] * HEADS, HW))
        mem_v = mem_v.reshape(mem_k.shape)
    mem_spec = pl.BlockSpec((None, None) + mem_k.shape[2:], lambda i: (layer, i // tiles_per_seq, 0, 0))
    return pl.pallas_call(
        functools.partial(_xattn_body, interleaved=interleaved),
        grid=(t // tm,),
        in_specs=[pl.BlockSpec((tm, d), lambda i: (i, 0)),
                  pl.BlockSpec((1, d), lambda i: (0, 0)),
                  pl.BlockSpec(wq.shape, lambda i: (0, 0)),
                  mem_spec, mem_spec,
                  pl.BlockSpec(wo.shape, lambda i: (0, 0))],
        out_specs=pl.BlockSpec((tm, d), lambda i: (i, 0)),
        out_shape=jax.ShapeDtypeStruct((t, d), F32),
        compiler_params=_params("parallel"),
        name="xattn",
    )(x, g.reshape(1, d), wq, mem_k, mem_v, wo)


def _ffn_body(x_ref, g_ref, w1_ref, w2_ref, o_ref, *, tf):
    x = x_ref[...]
    h = _rms(x, g_ref[...]).astype(BF16)
    o_ref[...] = x
    nf = w1_ref.shape[1] // tf
    up = lambda j: _dot(h, w1_ref[:, j * tf:(j + 1) * tf])
    nxt = up(0)
    for j in range(nf):
        a = jnp.maximum(nxt, 0.0)
        if j + 1 < nf:
            nxt = up(j + 1)
        o_ref[...] += _dot((a * a).astype(BF16), w2_ref[j * tf:(j + 1) * tf, :])


def ffn(x, g, w1, w2, tm, tf):
    t, d = x.shape
    resident = lambda a: pl.BlockSpec(a.shape, lambda i: (0, 0), pipeline_mode=pl.Buffered(1))
    return pl.pallas_call(
        functools.partial(_ffn_body, tf=tf),
        grid=(t // tm,),
        in_specs=[pl.BlockSpec((tm, d), lambda i: (i, 0)),
                  pl.BlockSpec((1, d), lambda i: (0, 0)),
                  resident(w1), resident(w2)],
        out_specs=pl.BlockSpec((tm, d), lambda i: (i, 0)),
        out_shape=jax.ShapeDtypeStruct((t, d), F32),
        compiler_params=_params("parallel"),
        name="ffn",
    )(x, g.reshape(1, d), w1, w2)


def _final_norm_body(x_ref, g_ref, o_ref):
    o_ref[...] = _rms(x_ref[...], g_ref[...])


def final_norm(x, g, tm):
    t, d = x.shape
    return pl.pallas_call(
        _final_norm_body,
        grid=(t // tm,),
        in_specs=[pl.BlockSpec((tm, d), lambda i: (i, 0)), pl.BlockSpec((1, d), lambda i: (0, 0))],
        out_specs=pl.BlockSpec((tm, d), lambda i: (i, 0)),
        out_shape=jax.ShapeDtypeStruct((t, d), F32),
        compiler_params=_params("parallel"),
        name="final_norm",
    )(x, g.reshape(1, d))


def _regroup_w_in(w):
    o = 0
    parts = {}
    for name, size in (("q", WIDTH), ("k", WIDTH), ("v", WIDTH), ("a", HEADS), ("b", HEADS), ("g", WIDTH),
                       ("dq", WIDTH), ("dk", WIDTH), ("dv", WIDTH), ("cu", WIDTH), ("cv", WIDTH),
                       ("gates", 3 * D_MODEL)):
        parts[name] = w[..., o:o + size].astype(BF16)
        o += size
    used = _Z_HALF + _ZB_AB + 2 * HEADS
    cols = [parts[n] for n in ("gates", "g", "cu", "q", "k", "v", "dq", "dk", "dv", "cv", "a", "b")]
    cols.append(jnp.zeros(w.shape[:-1] + (2 * _Z_HALF - used,), BF16))
    return jnp.concatenate(cols, axis=-1)


def _tile(n, want):
    while n % want:
        want //= 2
    return want


def kernel(x_prompt, x_sample, cache_diff_k, cache_diff_v, page_table, cache_mem_k, cache_mem_v, state_gdn, state_gdn_conv, mem_prompt, w_in, norm_mix, gdn_conv_w, gdn_A_log, gdn_dt_bias, gdn_o_norm, diff_lq1, diff_lk1, diff_lq2, diff_lk2, diff_subln, gmlp_ln_g, gmlp_ln_b, gmlp_ws, gmlp_b, w_br_gdn, w_br_diff, w_br_gmlp, w_out, norm_xa, norm_mem, xa_wq, xa_wk, xa_wv, xa_wo, norm_ffn, ffn_w1, ffn_w2, norm_final):
    depth = w_in.shape[0]
    bp, lp, d = x_prompt.shape
    bs, ls, _ = x_sample.shape
    mem_len = mem_prompt.shape[1]
    past_len = page_table.shape[1] * PAGE
    lpad = SAMPLE_PAD
    bf = lambda a: a.astype(BF16)

    mem_tok = mem_prompt.reshape(bp * mem_len, d)
    mem_kv = [norm_matmul(mem_tok, norm_mem[l], bf(jnp.concatenate([xa_wk[l], xa_wv[l]], axis=1)),
                          _tile(bp * mem_len, 512), 512) for l in range(depth)]
    mem_k_p = jnp.stack([kv[:, :WIDTH] for kv in mem_kv]).reshape(depth, bp, mem_len, WIDTH)
    mem_v_p = jnp.stack([kv[:, WIDTH:] for kv in mem_kv]).reshape(depth, bp, mem_len, WIDTH)

    w_in_all = _regroup_w_in(w_in)
    weights = [dict(wa=bf(w_br_gdn[l]), wb=bf(w_br_diff[l]), wc=bf(w_br_gmlp[l]),
                    wo=bf(w_out[l]), xq=bf(xa_wq[l]), xo=bf(xa_wo[l]), w1=bf(ffn_w1[l]), w2=bf(ffn_w2[l]))
               for l in range(depth)]

    def trunk(x, b, l, l_valid, pos, mem_k, mem_v, s_in, buf_in, paged):
        t = b * l
        prompt = paged is None
        act = BF16 if prompt else F32
        tm = _tile(t, 1024)
        tables = _rope_tables(pos)
        outs = dict(s=[], buf=[], gv=[])
        k_all = v_all = None
        for li in range(depth):
            w = weights[li]
            lam_init = 0.8 - 0.6 * math.exp(-0.3 * li)
            lparams = (diff_lq1[li], diff_lk1[li], diff_lq2[li], diff_lk2[li])
            za, zb = proj_in(x, norm_mix[li], w_in_all, li, _tile(t, 512), 1024, act)
            za3, zb3 = za.reshape(b, l, _Z_HALF), zb.reshape(b, l, _Z_HALF)
            o_a, s_new = gdn(za3, zb3, buf_in[li], s_in, li, gdn_conv_w[li], gdn_A_log[li], gdn_dt_bias[li],
                             gdn_o_norm[li], _tile(l, 256) if prompt else GDN_CHUNK, _tile(l, 256) if prompt else l,
                             l_valid, act)
            outs["buf"].append(zb3[:, l_valid - (GDN_CONV - 1):l_valid, _ZB_QKV:_ZB_QKV + GDN_CONV_CH])
            qm, k_all, kb, v_all, vb = rope_split(zb, tables, _tile(l, 512), act, li, depth, k_all, v_all)
            if prompt:
                o_b = flash_diff(qm, kb, vb, lparams, diff_subln[li], lam_init, b, _tile(l, 2048), _tile(l, 1024))
            else:
                ck, cv, pt = paged
                o_b = paged_diff(qm.reshape(2, b, l, WIDTH), kb.reshape(b, l, WIDTH), vb.reshape(b, l, WIDTH),
                                 ck, cv, pt, li, lparams, diff_subln[li], lam_init, l_valid)
                o_b = o_b.reshape(t, WIDTH)
            o_c, gv = gmlp(za, zb, gmlp_ln_g[li], gmlp_ln_b[li], gmlp_ws[li], gmlp_b[li], min(l, GMLP_CHUNK), act)
            x = merge(x, za, o_a.reshape(t, WIDTH), o_b, o_c, w["wa"], w["wb"], w["wc"], w["wo"], _tile(t, 512))
            xa_tm = _tile(l, 1024)
            x = xattn(x, norm_xa[li], w["xq"], mem_k, mem_v, li, w["xo"], xa_tm, l // xa_tm)
            x = ffn(x, norm_ffn[li], w["w1"], w["w2"], tm, 1024)
            outs["s"].append(s_new)
            outs["gv"].append(gv)
        y = final_norm(x, norm_final, tm)
        seq = lambda a: a.reshape((depth, b, l) + a.shape[2:])[:, :, :l_valid]
        return (y.reshape(b, l, d)[:, :l_valid], seq(k_all), seq(v_all),
                jnp.stack(outs["s"]), jnp.stack(outs["buf"]), seq(jnp.stack(outs["gv"])))

    s0_p = jnp.zeros((depth, bp, HEADS, HW, HW), F32)
    buf0_p = jnp.zeros((depth, bp, GDN_CONV - 1, GDN_CONV_CH), F32)
    (y_p, k_p, v_p, s_p, buf_p, _) = trunk(x_prompt.reshape(bp * lp, d), bp, lp, lp, jnp.arange(lp),
                                           mem_k_p, mem_v_p, s0_p, buf0_p, None)

    x_s = jnp.pad(x_sample, ((0, 0), (0, lpad - ls), (0, 0))).reshape(bs * lpad, d)
    (y_s, k_s, v_s, s_s, buf_s, gv_s) = trunk(x_s, bs, lpad, ls, past_len + jnp.arange(lpad),
                                              cache_mem_k, cache_mem_v, state_gdn, state_gdn_conv,
                                              (cache_diff_k, cache_diff_v, page_table))

    return (y_p, y_s, k_p, v_p,
            mem_k_p.reshape(depth, bp, mem_len, HEADS, HW), mem_v_p.reshape(depth, bp, mem_len, HEADS, HW),
            s_p, buf_p, k_s, v_s, s_s, buf_s, gv_s)
```

```python
import functools
import math

import jax
import jax.numpy as jnp
from jax import lax
from jax.experimental import pallas as pl
from jax.experimental.pallas import tpu as pltpu

F32 = jnp.float32
BF16 = jnp.bfloat16
EPS = 1e-6
NEG = -1e30

D_MODEL = 1024
HEADS = 4
HW = 128
WIDTH = HEADS * HW
DIFF_DH = 64
ROPE_DIM = 16
ROPE_THETA = 500000.0
GDN_CONV = 4
GDN_CHUNK = 64
GDN_CONV_CH = 3 * WIDTH
GMLP_CHUNK = 128
PAGE = 128
SAMPLE_PAD = 8

_ZA_GATES = 0
_ZA_GIN = 3 * D_MODEL
_ZA_CU = _ZA_GIN + WIDTH
_ZB_QKV = 0
_ZB_DQ = GDN_CONV_CH
_ZB_DK = _ZB_DQ + WIDTH
_ZB_DV = _ZB_DK + WIDTH
_ZB_CV = _ZB_DV + WIDTH
_ZB_AB = _ZB_CV + WIDTH
_Z_HALF = 4096

_VMEM_LIMIT = 56 * 1024 * 1024


def _params(*sem):
    return pltpu.CompilerParams(dimension_semantics=sem, vmem_limit_bytes=_VMEM_LIMIT)


def _rms(x, g):
    return x * lax.rsqrt(jnp.mean(x * x, axis=-1, keepdims=True) + EPS) * g


def _sigmoid(x):
    return 1.0 / (1.0 + jnp.exp(-x))


def _dot(a, b):
    return jnp.dot(a, b, preferred_element_type=F32)


def _dot_nt(a, b):
    return lax.dot_general(a, b, (((1,), (1,)), ((), ())), preferred_element_type=F32)


def _dot_tn(a, b):
    return lax.dot_general(a, b, (((0,), (0,)), ((), ())), preferred_element_type=F32)


def _split(a):
    hi = lax.bitcast_convert_type(lax.bitcast_convert_type(a, jnp.int32) & jnp.int32(-65536), F32)
    return hi, a - hi


def _hi_lhs(a):
    hi, lo = _split(a)
    return jnp.concatenate([hi, hi, lo], axis=1).astype(BF16)


def _hi_rhs(b):
    hi, lo = _split(b)
    return jnp.concatenate([hi, lo, hi], axis=0).astype(BF16)


def _dot_hi(a, b):
    return _dot(_hi_lhs(a), _hi_rhs(b))


def _dot_hi_exact_lhs(a_bf16, b):
    hi, lo = _split(b)
    lo_hi, lo_lo = _split(lo)
    return _dot(jnp.concatenate([a_bf16] * 3, axis=1), jnp.concatenate([hi, lo_hi, lo_lo], axis=0).astype(BF16))


def _dot_hi_exact_rhs(a, b_bf16):
    hi, lo = _split(a)
    lo_hi, lo_lo = _split(lo)
    return _dot(jnp.concatenate([hi, lo_hi, lo_lo], axis=1).astype(BF16), jnp.concatenate([b_bf16] * 3, axis=0))


def _norm_mm_body(x_ref, g_ref, w_ref, o_ref, h_ref):
    @pl.when(pl.program_id(1) == 0)
    def _():
        h_ref[...] = _rms(x_ref[...], g_ref[...]).astype(BF16)

    o_ref[...] = _dot(h_ref[...], w_ref[...]).astype(o_ref.dtype)


def norm_matmul(x, g, w, tm, tn):
    t, d = x.shape
    n = w.shape[1]
    return pl.pallas_call(
        _norm_mm_body,
        grid=(t // tm, n // tn),
        in_specs=[pl.BlockSpec((tm, d), lambda i, j: (i, 0)),
                  pl.BlockSpec((1, d), lambda i, j: (0, 0)),
                  pl.BlockSpec((d, tn), lambda i, j: (0, j))],
        out_specs=pl.BlockSpec((tm, tn), lambda i, j: (i, j)),
        out_shape=jax.ShapeDtypeStruct((t, n), F32),
        scratch_shapes=[pltpu.VMEM((tm, d), BF16)],
        compiler_params=_params("parallel", "arbitrary"),
        name="norm_matmul",
    )(x, g.reshape(1, d), w)


def _proj_in_body(x_ref, g_ref, w_ref, oa_ref, ob_ref, *, tn):
    h = _rms(x_ref[...], g_ref[...]).astype(BF16)
    for j in range(_Z_HALF // tn):
        oa_ref[:, j * tn:(j + 1) * tn] = _dot(h, w_ref[:, j * tn:(j + 1) * tn]).astype(oa_ref.dtype)
    for j in range(_Z_HALF // tn):
        ob_ref[:, j * tn:(j + 1) * tn] = _dot(h, w_ref[:, _Z_HALF + j * tn:_Z_HALF + (j + 1) * tn])


def proj_in(x, g, w_all, layer, tm, tn, act_dtype):
    t, d = x.shape
    return pl.pallas_call(
        functools.partial(_proj_in_body, tn=tn),
        grid=(t // tm,),
        in_specs=[pl.BlockSpec((tm, d), lambda i: (i, 0)),
                  pl.BlockSpec((1, d), lambda i: (0, 0)),
                  pl.BlockSpec((None, d, 2 * _Z_HALF), lambda i: (layer, 0, 0), pipeline_mode=pl.Buffered(1))],
        out_specs=[pl.BlockSpec((tm, _Z_HALF), lambda i: (i, 0)),
                   pl.BlockSpec((tm, _Z_HALF), lambda i: (i, 0))],
        out_shape=[jax.ShapeDtypeStruct((t, _Z_HALF), act_dtype), jax.ShapeDtypeStruct((t, _Z_HALF), F32)],
        compiler_params=_params("parallel"),
        name="proj_in",
    )(x, g.reshape(1, d), w_all)


def _gdn_body(qkv_ref, gin_ref, ab_ref, buf_ref, s0_ref, cw_ref, alog_ref, dtb_ref, onorm_ref,
              o_ref, sout_ref, xs_ref, ab_scr, s_ref, *, tl, tl_in, valid):
    c = GDN_CHUNK
    nc = tl // c
    nb = qkv_ref.shape[0]
    t = pl.program_id(1)

    @pl.when(t == 0)
    def _():
        s_ref[...] = s0_ref[...]
        xs_ref[...] = jnp.zeros_like(xs_ref)
        xs_ref[:, 5:8, :] = buf_ref[...]
        ab_scr[...] = jnp.zeros_like(ab_scr)

    ti = lax.broadcasted_iota(jnp.int32, (tl, tl), 0)
    tj = lax.broadcasted_iota(jnp.int32, (tl, tl), 1)
    same_chunk = (ti // c) == (tj // c)
    sum_rows = jnp.where(same_chunk & (ti >= tj), 1.0, 0.0).astype(BF16)
    sum_lanes = jnp.where(same_chunk & (ti <= tj), 1.0, 0.0).astype(BF16)
    w = cw_ref[...]

    def front(bi):
        xs_ref[bi, 8:8 + tl_in, :] = qkv_ref[bi]
        ab_scr[bi, 0:tl_in, :] = ab_ref[bi]
        xs = xs_ref[bi]
        y = xs[8:] * w[3:4]
        for i in range(1, GDN_CONV):
            y = y + pltpu.roll(xs, i, 0)[8:] * w[3 - i:4 - i]
        xs_ref[bi, 5:8, :] = xs_ref[bi, tl + 5:tl + 8, :]
        y = y * _sigmoid(y)
        ab = ab_scr[bi]
        sp_in = ab + dtb_ref[...]
        softplus = jnp.maximum(sp_in, 0.0) + jnp.log1p(jnp.exp(-jnp.abs(sp_in)))
        g = -jnp.exp(alog_ref[...]) * softplus
        beta = _sigmoid(ab)
        if valid < tl:
            live = lax.broadcasted_iota(jnp.int32, (tl, 1), 0) < valid
            g = jnp.where(live, g, 0.0)
            beta = jnp.where(live, beta, 0.0)
        gcol = _dot_hi_exact_lhs(sum_rows, g)
        grow = _dot_hi_exact_rhs(g.T[0:8, :], sum_lanes)
        return y, beta, gcol, grow

    fronts = [front(bi) for bi in range(nb)]

    ii = lax.broadcasted_iota(jnp.int32, (c, c), 0)
    jj = lax.broadcasted_iota(jnp.int32, (c, c), 1)
    incl = ii >= jj
    strict = ii > jj
    eye = (ii == jj).astype(F32)

    def prepare(bi, ci, h):
        y, beta, gcol, grow = fronts[bi]
        rs = slice(ci * c, (ci + 1) * c)
        gi = gcol[rs, h:h + 1]
        gj = grow[h:h + 1, rs]
        dec_incl = jnp.exp(jnp.where(incl, gi - gj, NEG))
        glast = gcol[(ci + 1) * c - 1:(ci + 1) * c, h:h + 1]
        qh = y[rs, h * HW:(h + 1) * HW]
        kh = y[rs, WIDTH + h * HW:WIDTH + (h + 1) * HW]
        qh = qh * lax.rsqrt(jnp.sum(qh * qh, axis=-1, keepdims=True) + EPS) * (HW ** -0.5)
        kh = kh * lax.rsqrt(jnp.sum(kh * kh, axis=-1, keepdims=True) + EPS)
        return dict(dec_incl=dec_incl, dec_strict=jnp.where(strict, dec_incl, 0.0), gexp=jnp.exp(gi),
                    bcol=beta[rs, 4 + h:5 + h], kh=kh, vh=y[rs, 2 * WIDTH + h * HW:2 * WIDTH + (h + 1) * HW],
                    qb=qh.astype(BF16), kb=kh.astype(BF16), tail=jnp.exp(glast - gi), gend=jnp.exp(glast))

    terms = [prepare(bi, ci, h) for ci in range(nc) for bi in range(nb) for h in range(HEADS)]
    for m in terms:
        kk = _dot_nt(m["kb"], m["kb"])
        m["p"] = (_dot_nt(m["qb"], m["kb"]) * m["dec_incl"]).astype(BF16)
        m["x"] = -(m["bcol"] * kk * m["dec_strict"])
        m["tinv"] = eye + m["x"]
    for m in terms:
        m["x"] = _dot_hi(m["x"], m["x"])
    for _ in range(int(math.log2(c)) - 2):
        for m in terms:
            r = _dot_hi(jnp.concatenate([m["x"], m["tinv"]], axis=0), m["x"])
            m["x"] = r[0:c]
            m["tinv"] = m["tinv"] + r[c:2 * c]
    for m in terms:
        m["tinv"] = m["tinv"] + _dot_hi(m["tinv"], m["x"])
    for m in terms:
        sol = _dot_hi(m["tinv"], jnp.concatenate([m["bcol"] * m["vh"], (m["bcol"] * m["gexp"]) * m["kh"]], axis=-1))
        m["u0"] = sol[:, :HW]
        m["w"] = sol[:, HW:].astype(BF16)

    seq_heads = [(bi, h) for bi in range(nb) for h in range(HEADS)]
    state = [s_ref[bi, h] for bi, h in seq_heads]
    for ci in range(nc):
        ms = terms[ci * len(seq_heads):(ci + 1) * len(seq_heads)]
        sbs = [s.astype(BF16) for s in state]
        us = [m["u0"] - _dot_nt(m["w"], sb) for m, sb in zip(ms, sbs)]
        qs = [_dot_nt(m["qb"], sb) for m, sb in zip(ms, sbs)]
        pus = [_dot(m["p"], u.astype(BF16)) for m, u in zip(ms, us)]
        state = [m["gend"] * s + _dot_tn((u * m["tail"]).astype(BF16), m["kb"])
                 for m, s, u in zip(ms, state, us)]
        r0 = ci * c
        rows = min(c, tl_in - r0)
        if rows > 0:
            for (bi, h), m, qs_h, pu in zip(seq_heads, ms, qs, pus):
                hs = slice(h * HW, (h + 1) * HW)
                o = _rms(m["gexp"] * qs_h + pu, onorm_ref[...])[0:rows]
                gin = gin_ref[bi, r0:r0 + rows, hs].astype(F32)
                o_ref[bi, r0:r0 + rows, hs] = (o * (gin * _sigmoid(gin))).astype(o_ref.dtype)
    for (bi, h), s in zip(seq_heads, state):
        s_ref[bi, h] = s

    @pl.when(t == pl.num_programs(1) - 1)
    def _():
        sout_ref[...] = s_ref[...]


def gdn(za3, zb3, conv_buf, s0, layer, conv_w, a_log, dt_bias, o_norm, tl, tl_in, valid, act_dtype):
    b, l, _ = zb3.shape
    nt = l // tl_in
    nb = 4 if (nt == 1 and b % 4 == 0) else 1
    assert tl % GDN_CHUNK == 0 and (tl_in == tl or nt == 1)
    lane_pad = lambda v, off: jnp.zeros((1, HW), F32).at[0, off:off + HEADS].set(v)
    body = functools.partial(_gdn_body, tl=tl, tl_in=tl_in, valid=valid)
    return pl.pallas_call(
        body,
        grid=(b // nb, nt),
        in_specs=[pl.BlockSpec((nb, tl_in, GDN_CONV_CH), lambda i, t: (i, t, _ZB_QKV // GDN_CONV_CH)),
                  pl.BlockSpec((nb, tl_in, WIDTH), lambda i, t: (i, t, _ZA_GIN // WIDTH)),
                  pl.BlockSpec((nb, tl_in, HW), lambda i, t: (i, t, _ZB_AB // HW)),
                  pl.BlockSpec((nb, GDN_CONV - 1, GDN_CONV_CH), lambda i, t: (i, 0, 0)),
                  pl.BlockSpec((None, nb, HEADS, HW, HW), lambda i, t: (layer, i, 0, 0, 0)),
                  pl.BlockSpec((GDN_CONV, GDN_CONV_CH), lambda i, t: (0, 0)),
                  pl.BlockSpec((1, HW), lambda i, t: (0, 0)),
                  pl.BlockSpec((1, HW), lambda i, t: (0, 0)),
                  pl.BlockSpec((1, HW), lambda i, t: (0, 0))],
        out_specs=[pl.BlockSpec((nb, tl_in, WIDTH), lambda i, t: (i, t, 0)),
                   pl.BlockSpec((nb, HEADS, HW, HW), lambda i, t: (i, 0, 0, 0))],
        out_shape=[jax.ShapeDtypeStruct((b, l, WIDTH), act_dtype),
                   jax.ShapeDtypeStruct((b, HEADS, HW, HW), F32)],
        scratch_shapes=[pltpu.VMEM((nb, tl + 8, GDN_CONV_CH), F32),
                        pltpu.VMEM((nb, tl, HW), F32),
                        pltpu.VMEM((nb, HEADS, HW, HW), F32)],
        compiler_params=_params("parallel", "arbitrary"),
        name="gdn",
    )(zb3, za3, zb3, conv_buf, s0, conv_w, lane_pad(a_log, 0), lane_pad(dt_bias, 0), o_norm.reshape(1, HW))


def _rope_tables(pos):
    half = ROPE_DIM // 2
    inv = ROPE_THETA ** (-jnp.arange(0, ROPE_DIM, 2, dtype=F32) / ROPE_DIM)
    ang = pos.astype(F32)[:, None] * inv[None, :]
    m = jnp.arange(HW) % DIFF_DH
    cos = jnp.cos(ang)[:, m % half]
    sin = jnp.sin(ang)[:, m % half]
    cos_t = jnp.where(m < ROPE_DIM, cos, 1.0)
    sin_a = jnp.where(m < half, -sin, 0.0)
    sin_b = jnp.where((m >= half) & (m < ROPE_DIM), sin, 0.0)
    return cos_t.astype(F32), sin_a.astype(F32), sin_b.astype(F32)


def _rope_body(q_ref, k_ref, v_ref, cos_ref, sa_ref, sb_ref, *refs):
    qm_ref, kr_ref, kb_ref, vf_ref, vb_ref = refs[-5:]
    half = ROPE_DIM // 2
    cos, sa, sb = cos_ref[...], sa_ref[...], sb_ref[...]
    lane = lax.broadcasted_iota(jnp.int32, (1, HW), 1)
    first_map = lane < DIFF_DH

    def rot(x):
        return x * cos + pltpu.roll(x, HW - half, 1) * sa + pltpu.roll(x, half, 1) * sb

    for h in range(HEADS):
        hs = slice(h * HW, (h + 1) * HW)
        q = rot(q_ref[:, hs]) * (DIFF_DH ** -0.5)
        qm_ref[0, :, hs] = jnp.where(first_map, q, 0.0).astype(qm_ref.dtype)
        qm_ref[1, :, hs] = jnp.where(first_map, 0.0, q).astype(qm_ref.dtype)
        k = rot(k_ref[:, hs])
        kr_ref[:, h, :] = k
        kb_ref[:, hs] = k.astype(kb_ref.dtype)
        vf_ref[:, h, :] = v_ref[:, hs]
    vb_ref[...] = v_ref[...].astype(vb_ref.dtype)


def rope_split(z, tables, tr, act_dtype, layer, depth, k_all, v_all):
    t = z.shape[0]
    nl = tables[0].shape[0] // tr
    zspec = lambda off: pl.BlockSpec((tr, WIDTH), lambda i: (i, off // WIDTH))
    tspec = pl.BlockSpec((tr, HW), lambda i: (i % nl, 0))
    ospec = pl.BlockSpec((tr, WIDTH), lambda i: (i, 0))
    hspec = pl.BlockSpec((None, tr, HEADS, HW), lambda i: (layer, i, 0, 0))
    carried = [] if k_all is None else [k_all, v_all]
    return pl.pallas_call(
        _rope_body,
        grid=(t // tr,),
        in_specs=[zspec(_ZB_DQ), zspec(_ZB_DK), zspec(_ZB_DV), tspec, tspec, tspec]
                 + [pl.BlockSpec(memory_space=pl.ANY)] * len(carried),
        out_specs=[pl.BlockSpec((2, tr, WIDTH), lambda i: (0, i, 0)), hspec, ospec, hspec, ospec],
        out_shape=[jax.ShapeDtypeStruct((2, t, WIDTH), act_dtype),
                   jax.ShapeDtypeStruct((depth, t, HEADS, HW), F32),
                   jax.ShapeDtypeStruct((t, WIDTH), act_dtype),
                   jax.ShapeDtypeStruct((depth, t, HEADS, HW), F32),
                   jax.ShapeDtypeStruct((t, WIDTH), act_dtype)],
        input_output_aliases={6: 1, 7: 3} if carried else {},
        compiler_params=_params("parallel"),
        name="rope_split",
    )(z, z, z, *tables, *carried)


def _lambda(lq1_ref, lk1_ref, lq2_ref, lk2_ref, lam_init):
    s1 = jnp.sum(lq1_ref[...] * lk1_ref[...], axis=-1, keepdims=True)
    s2 = jnp.sum(lq2_ref[...] * lk2_ref[...], axis=-1, keepdims=True)
    return jnp.exp(s1) - jnp.exp(s2) + lam_init


_FLASH_ROWS = 256


def _flash_body(qi_ref, ki_ref, q_ref, k_ref, v_ref, lq1_ref, lk1_ref, lq2_ref, lk2_ref, sub_ref, o_ref,
                m_ref, acc_ref, *, lam_init):
    pair = pl.program_id(2)
    qi = qi_ref[pair]
    ki = ki_ref[pair]
    tq = q_ref.shape[1]
    tk = k_ref.shape[0]
    ratio = tq // tk
    rows = min(tq, _FLASH_ROWS)

    @pl.when(ki == 0)
    def _():
        m_ref[...] = jnp.full_like(m_ref, NEG)
        acc_ref[...] = jnp.zeros_like(acc_ref)

    def block(diag):
        k = k_ref[...]
        v = v_ref[...]
        v1 = jnp.concatenate([v, jnp.ones_like(v)], axis=1)
        col0 = 0 if diag is None else diag * tk
        chains = [(c, r0) for c in range(2) for r0 in range(0, tq, rows) if r0 + rows > col0]

        def n_keys(r0):
            return tk if diag is None else min(tk, r0 + rows - col0)

        def scores(c, r0):
            kc = n_keys(r0)
            s = _dot_nt(q_ref[c, r0:r0 + rows, :], k[0:kc])
            if diag is not None and r0 < col0 + kc - 1:
                ri = lax.broadcasted_iota(jnp.int32, s.shape, 0) + r0
                ci = lax.broadcasted_iota(jnp.int32, s.shape, 1) + col0
                s = jnp.where(ci <= ri, s, NEG)
            return s

        ahead = 3
        pending = [scores(*ch) for ch in chains[:ahead]]
        for i, (c, r0) in enumerate(chains):
            rs = slice(r0, r0 + rows)
            s = pending.pop(0)
            kc = n_keys(r0)
            tiles = [s[:, j * HW:(j + 1) * HW] for j in range(kc // HW)]
            tile_max = functools.reduce(jnp.maximum, tiles)
            m_prev = m_ref[c, rs, :]
            m_new = jnp.maximum(m_prev, jnp.max(tile_max, axis=-1, keepdims=True))
            alpha = jnp.exp(m_prev - m_new)
            p = jnp.concatenate([jnp.exp(tl - m_new).astype(BF16) for tl in tiles], axis=1)
            pv = _dot(p, v1[0:kc])
            if i + ahead < len(chains):
                pending.append(scores(*chains[i + ahead]))
            acc_ref[c, rs, :] = jnp.concatenate([alpha, alpha], axis=1) * acc_ref[c, rs, :] + pv
            m_ref[c, rs, :] = m_new

    pl.when(ki < qi * ratio)(functools.partial(block, None))
    for d in range(ratio):
        pl.when(ki == qi * ratio + d)(functools.partial(block, d))

    @pl.when(ki == qi * ratio + ratio - 1)
    def _():
        lam = _lambda(lq1_ref, lk1_ref, lq2_ref, lk2_ref, lam_init)
        o = (acc_ref[0, :, 0:HW] * (1.0 / acc_ref[0, :, HW:2 * HW])
             - lam * (acc_ref[1, :, 0:HW] * (1.0 / acc_ref[1, :, HW:2 * HW])))
        o_ref[...] = (_rms(o, sub_ref[...]) * (1.0 - lam_init)).astype(o_ref.dtype)


def flash_diff(qm, kb, vb, lparams, subln, lam_init, batch, tq, tk):
    t = kb.shape[0]
    l = t // batch
    nq = l // tq
    nk = l // tk
    assert tq % tk == 0
    pairs = [(i, j) for i in range(nq) for j in range((i + 1) * (tq // tk))]
    qi_arr = jnp.asarray([p[0] for p in pairs], jnp.int32)
    ki_arr = jnp.asarray([p[1] for p in pairs], jnp.int32)
    lspec = pl.BlockSpec((1, DIFF_DH), lambda b, h, p, qi, ki: (0, 0))
    kvspec = pl.BlockSpec((tk, HW), lambda b, h, p, qi, ki: (b * nk + ki[p], h))
    grid_spec = pltpu.PrefetchScalarGridSpec(
        num_scalar_prefetch=2,
        grid=(batch, HEADS, len(pairs)),
        in_specs=[pl.BlockSpec((2, tq, HW), lambda b, h, p, qi, ki: (0, b * nq + qi[p], h)),
                  kvspec, kvspec, lspec, lspec, lspec, lspec,
                  pl.BlockSpec((1, HW), lambda b, h, p, qi, ki: (0, 0))],
        out_specs=pl.BlockSpec((tq, HW), lambda b, h, p, qi, ki: (b * nq + qi[p], h)),
        scratch_shapes=[pltpu.VMEM((2, tq, HW), F32), pltpu.VMEM((2, tq, 2 * HW), F32)],
    )
    return pl.pallas_call(
        functools.partial(_flash_body, lam_init=lam_init),
        grid_spec=grid_spec,
        out_shape=jax.ShapeDtypeStruct((t, WIDTH), BF16),
        compiler_params=_params("parallel", "parallel", "arbitrary"),
        name="flash_diff",
    )(qi_arr, ki_arr, qm, kb, vb, *[p.reshape(1, DIFF_DH) for p in lparams], subln.reshape(1, HW))


_PAGES_PER_STEP = 32


def _paged_body(pt_ref, q_ref, *refs, lam_init, n_valid):
    del pt_ref
    pp = _PAGES_PER_STEP
    k_refs, v_refs = refs[:pp], refs[pp:2 * pp]
    (kn_ref, vn_ref, lq1_ref, lk1_ref, lq2_ref, lk2_ref, sub_ref, o_ref,
     m_ref, l_ref, acc_ref, kn_scr, vn_scr) = refs[2 * pp:]
    p_idx = pl.program_id(1)
    nq = q_ref.shape[1]

    @pl.when(p_idx == 0)
    def _():
        m_ref[...] = jnp.full_like(m_ref, NEG)
        l_ref[...] = jnp.zeros_like(l_ref)
        acc_ref[...] = jnp.zeros_like(acc_ref)

    def head_q(h):
        hs = slice(h * HW, (h + 1) * HW)
        return jnp.concatenate([q_ref[0, :, hs], q_ref[1, :, hs]], axis=0).astype(BF16)

    def update(k_page, v_page, n_pages, mask):
        qs = [head_q(h) for h in range(HEADS)]
        s = jnp.concatenate(
            [jnp.concatenate([_dot_nt(qs[h], k_page(j, h).astype(BF16)) for j in range(n_pages)], axis=-1)
             for h in range(HEADS)], axis=0)
        if mask is not None:
            s = jnp.where(mask, s, NEG)
        m_prev = m_ref[...]
        m_new = jnp.maximum(m_prev, jnp.max(s, axis=-1, keepdims=True))
        alpha = jnp.exp(m_prev - m_new)
        p = jnp.exp(s - m_new).astype(BF16)
        l_ref[...] = alpha * l_ref[...] + jnp.sum(p.astype(F32), axis=-1, keepdims=True)
        pvs = []
        for h in range(HEADS):
            ph = p[h * 2 * nq:(h + 1) * 2 * nq]
            pv = _dot(ph[:, 0:PAGE], v_page(0, h).astype(BF16))
            for j in range(1, n_pages):
                pv = pv + _dot(ph[:, j * PAGE:(j + 1) * PAGE], v_page(j, h).astype(BF16))
            pvs.append(pv)
        acc_ref[...] = alpha * acc_ref[...] + jnp.concatenate(pvs, axis=0)
        m_ref[...] = m_new

    head_rows = lambda h: pl.ds(h, PAGE, stride=HEADS)
    update(lambda j, h: k_refs[j][head_rows(h), :], lambda j, h: v_refs[j][head_rows(h), :], pp, None)

    @pl.when(p_idx == pl.num_programs(1) - 1)
    def _():
        kn_scr[...] = jnp.zeros_like(kn_scr)
        vn_scr[...] = jnp.zeros_like(vn_scr)
        kn_scr[0:nq, :] = kn_ref[...]
        vn_scr[0:nq, :] = vn_ref[...]
        ri = lax.broadcasted_iota(jnp.int32, (HEADS * 2 * nq, PAGE), 0) % nq
        ci = lax.broadcasted_iota(jnp.int32, (HEADS * 2 * nq, PAGE), 1)
        update(lambda j, h: kn_scr[:, h * HW:(h + 1) * HW], lambda j, h: vn_scr[:, h * HW:(h + 1) * HW], 1,
               (ci <= ri) & (ci < n_valid))
        lam = _lambda(lq1_ref, lk1_ref, lq2_ref, lk2_ref, lam_init)
        for h in range(HEADS):
            hs = slice(h * HW, (h + 1) * HW)
            r0 = slice(h * 2 * nq, h * 2 * nq + nq)
            r1 = slice(h * 2 * nq + nq, (h + 1) * 2 * nq)
            o = acc_ref[r0] * (1.0 / l_ref[r0]) - lam * (acc_ref[r1] * (1.0 / l_ref[r1]))
            o_ref[:, hs] = (_rms(o, sub_ref[...]) * (1.0 - lam_init)).astype(o_ref.dtype)


def paged_diff(qm, kb, vb, cache_k, cache_v, page_table, layer, lparams, subln, lam_init, n_valid):
    _, b, nq, _ = qm.shape
    n_pages = page_table.shape[1]
    pp = _PAGES_PER_STEP
    assert n_pages % pp == 0
    depth, n_pool = cache_k.shape[:2]
    cache_k = cache_k.reshape(depth, n_pool, PAGE * HEADS, HW)
    cache_v = cache_v.reshape(depth, n_pool, PAGE * HEADS, HW)
    page_spec = lambda j: pl.BlockSpec((None, None, PAGE * HEADS, HW),
                                       lambda i, p, pt: (layer, pt[i, p * pp + j], 0, 0))
    page_specs = [page_spec(j) for j in range(pp)]
    new_spec = pl.BlockSpec((None, nq, WIDTH), lambda i, p, pt: (i, 0, 0))
    lspec = pl.BlockSpec((1, DIFF_DH), lambda i, p, pt: (0, 0))
    grid_spec = pltpu.PrefetchScalarGridSpec(
        num_scalar_prefetch=1,
        grid=(b, n_pages // pp),
        in_specs=([pl.BlockSpec((2, None, nq, WIDTH), lambda i, p, pt: (0, i, 0, 0))]
                  + page_specs + page_specs
                  + [new_spec, new_spec, lspec, lspec, lspec, lspec,
                     pl.BlockSpec((1, HW), lambda i, p, pt: (0, 0))]),
        out_specs=pl.BlockSpec((None, nq, WIDTH), lambda i, p, pt: (i, 0, 0)),
        scratch_shapes=[pltpu.VMEM((HEADS * 2 * nq, 1), F32), pltpu.VMEM((HEADS * 2 * nq, 1), F32),
                        pltpu.VMEM((HEADS * 2 * nq, HW), F32),
                        pltpu.VMEM((PAGE, WIDTH), F32), pltpu.VMEM((PAGE, WIDTH), F32)],
    )
    return pl.pallas_call(
        functools.partial(_paged_body, lam_init=lam_init, n_valid=n_valid),
        grid_spec=grid_spec,
        out_shape=jax.ShapeDtypeStruct((b, nq, WIDTH), F32),
        compiler_params=_params("parallel", "arbitrary"),
        name="paged_diff",
    )(page_table, qm, *([cache_k] * pp), *([cache_v] * pp), kb, vb,
      *[p.reshape(1, DIFF_DH) for p in lparams], subln.reshape(1, HW))


def _gelu(x):
    return 0.5 * x * (1.0 + jnp.tanh(math.sqrt(2.0 / math.pi) * (x + 0.044715 * (x * x * x))))


def _gmlp_body(u_ref, v_ref, lg_ref, lb_ref, ws_ref, bst_ref, o_ref, gv_ref, v_scr, *, tr):
    u = _gelu(u_ref[...].astype(F32))
    v = _gelu(v_ref[...])
    mu = jnp.mean(v, axis=-1, keepdims=True)
    vc = v - mu
    v = vc * lax.rsqrt(jnp.mean(vc * vc, axis=-1, keepdims=True) + EPS) * lg_ref[...] + lb_ref[...]
    gv_ref[...] = v
    if tr < GMLP_CHUNK:
        v_scr[...] = jnp.zeros_like(v_scr)
        v_scr[0:tr, :] = v
    ii = lax.broadcasted_iota(jnp.int32, (tr, GMLP_CHUNK), 0)
    jj = lax.broadcasted_iota(jnp.int32, (tr, GMLP_CHUNK), 1)
    for g in range(HEADS):
        gs = slice(g * HW, (g + 1) * HW)
        w = jnp.where(ii >= jj, ws_ref[g, 0:tr, :], 0.0).astype(BF16)
        bias = bst_ref[0:tr, g:g + 1]
        for r0 in range(0, u.shape[0], tr):
            vg = (v_scr[:, gs] if tr < GMLP_CHUNK else v[r0:r0 + tr, gs]).astype(BF16)
            o_ref[r0:r0 + tr, gs] = (u[r0:r0 + tr, gs] * (_dot(w, vg) + bias)).astype(o_ref.dtype)


def gmlp(za, zb, ln_g, ln_b, ws, bs, tr, act_dtype):
    t = zb.shape[0]
    tm = _tile(t, 4 * tr) if tr == GMLP_CHUNK else tr
    zspec = lambda off: pl.BlockSpec((tm, WIDTH), lambda i: (i, off // WIDTH))
    ospec = pl.BlockSpec((tm, WIDTH), lambda i: (i, 0))
    vec = pl.BlockSpec((1, WIDTH), lambda i: (0, 0))
    return pl.pallas_call(
        functools.partial(_gmlp_body, tr=tr),
        grid=(t // tm,),
        in_specs=[zspec(_ZA_CU), zspec(_ZB_CV), vec, vec,
                  pl.BlockSpec((HEADS, GMLP_CHUNK, GMLP_CHUNK), lambda i: (0, 0, 0)),
                  pl.BlockSpec((GMLP_CHUNK, HEADS), lambda i: (0, 0))],
        out_specs=[ospec, ospec],
        out_shape=[jax.ShapeDtypeStruct((t, WIDTH), act_dtype), jax.ShapeDtypeStruct((t, WIDTH), F32)],
        scratch_shapes=[pltpu.VMEM((GMLP_CHUNK, WIDTH), F32)],
        compiler_params=_params("parallel"),
        name="gmlp",
    )(za, zb, ln_g.reshape(1, WIDTH), ln_b.reshape(1, WIDTH), ws, bs.T)


def _merge_body(x_ref, gate_ref, oa_ref, ob_ref, oc_ref, wa_ref, wb_ref, wc_ref, wo_ref, o_ref):
    d = D_MODEL
    gate = lambda i: _sigmoid(gate_ref[:, i * d:(i + 1) * d].astype(F32))
    merged = (gate(0) * _dot(oa_ref[...].astype(BF16), wa_ref[...])
              + gate(1) * _dot(ob_ref[...].astype(BF16), wb_ref[...])
              + gate(2) * _dot(oc_ref[...].astype(BF16), wc_ref[...]))
    o_ref[...] = x_ref[...] + _dot(merged.astype(BF16), wo_ref[...])


def merge(x, z, oa, ob, oc, wa, wb, wc, wo, tm):
    t, d = x.shape
    row = lambda w: pl.BlockSpec((tm, w), lambda i: (i, 0))
    full = lambda a: pl.BlockSpec(a.shape, lambda i: (0, 0))
    return pl.pallas_call(
        _merge_body,
        grid=(t // tm,),
        in_specs=[row(d), row(3 * d), row(WIDTH), row(WIDTH), row(WIDTH),
                  full(wa), full(wb), full(wc), full(wo)],
        out_specs=row(d),
        out_shape=jax.ShapeDtypeStruct((t, d), F32),
        compiler_params=_params("parallel"),
        name="merge",
    )(x, z, oa, ob, oc, wa, wb, wc, wo)


def _xattn_body(x_ref, g_ref, wq_ref, mk_ref, mv_ref, wo_ref, o_ref, *, interleaved):
    x = x_ref[...]
    q = _dot(_rms(x, g_ref[...]).astype(BF16), wq_ref[...])
    ns = mk_ref.shape[0]
    rows = x.shape[0] // ns
    if interleaved:
        m = mk_ref.shape[1] // HEADS
        head = lambda ref, si, h: ref[si, pl.ds(h, m, stride=HEADS), :]
    else:
        head = lambda ref, si, h: ref[si, :, h * HW:(h + 1) * HW]
    pairs = [(si, h) for si in range(ns) for h in range(HEADS)]
    ss = [_dot_nt(q[si * rows:(si + 1) * rows, h * HW:(h + 1) * HW].astype(BF16),
                  head(mk_ref, si, h).astype(BF16)) * (HW ** -0.5) for si, h in pairs]
    es = [jnp.exp(s - jnp.max(s, axis=-1, keepdims=True)) for s in ss]
    ps = [(e * (1.0 / jnp.sum(e, axis=-1, keepdims=True))).astype(BF16) for e in es]
    os_ = [_dot(p, head(mv_ref, si, h).astype(BF16)) for (si, h), p in zip(pairs, ps)]
    o = jnp.concatenate([jnp.concatenate(os_[si * HEADS:(si + 1) * HEADS], axis=-1) for si in range(ns)], axis=0)
    o_ref[...] = x + _dot(o.astype(BF16), wo_ref[...])


def xattn(x, g, wq, mem_k, mem_v, layer, wo, tm, tiles_per_seq, seqs_per_tile=1):
    t, d = x.shape
    assert tiles_per_seq == 1 or seqs_per_tile == 1
    interleaved = mem_k.ndim == 5
    if interleaved:
        mem_k = mem_k.reshape(mem_k.shape[:2] + (mem_k.shape[2] * HEADS, HW))
        mem_v = mem_v.reshape(mem_k.shape)
    mem_spec = pl.BlockSpec((None, seqs_per_tile) + mem_k.shape[2:], lambda i: (layer, i // tiles_per_seq, 0, 0))
    return pl.pallas_call(
        functools.partial(_xattn_body, interleaved=interleaved),
        grid=(t // tm,),
        in_specs=[pl.BlockSpec((tm, d), lambda i: (i, 0)),
                  pl.BlockSpec((1, d), lambda i: (0, 0)),
                  pl.BlockSpec(wq.shape, lambda i: (0, 0)),
                  mem_spec, mem_spec,
                  pl.BlockSpec(wo.shape, lambda i: (0, 0))],
        out_specs=pl.BlockSpec((tm, d), lambda i: (i, 0)),
        out_shape=jax.ShapeDtypeStruct((t, d), F32),
        compiler_params=_params("parallel"),
        name="xattn",
    )(x, g.reshape(1, d), wq, mem_k, mem_v, wo)


def _ffn_body(x_ref, g_ref, w1_ref, w2_ref, o_ref, *, tf):
    x = x_ref[...]
    h = _rms(x, g_ref[...]).astype(BF16)
    o_ref[...] = x
    nf = w1_ref.shape[1] // tf
    up = lambda j: _dot(h, w1_ref[:, j * tf:(j + 1) * tf])
    nxt = up(0)
    for j in range(nf):
        a = jnp.maximum(nxt, 0.0)
        if j + 1 < nf:
            nxt = up(j + 1)
        o_ref[...] += _dot((a * a).astype(BF16), w2_ref[j * tf:(j + 1) * tf, :])


def ffn(x, g, w1, w2, tm, tf):
    t, d = x.shape
    resident = lambda a: pl.BlockSpec(a.shape, lambda i: (0, 0), pipeline_mode=pl.Buffered(1))
    return pl.pallas_call(
        functools.partial(_ffn_body, tf=tf),
        grid=(t // tm,),
        in_specs=[pl.BlockSpec((tm, d), lambda i: (i, 0)),
                  pl.BlockSpec((1, d), lambda i: (0, 0)),
                  resident(w1), resident(w2)],
        out_specs=pl.BlockSpec((tm, d), lambda i: (i, 0)),
        out_shape=jax.ShapeDtypeStruct((t, d), F32),
        compiler_params=_params("parallel"),
        name="ffn",
    )(x, g.reshape(1, d), w1, w2)


def _final_norm_body(x_ref, g_ref, o_ref):
    o_ref[...] = _rms(x_ref[...], g_ref[...])


def final_norm(x, g, tm):
    t, d = x.shape
    return pl.pallas_call(
        _final_norm_body,
        grid=(t // tm,),
        in_specs=[pl.BlockSpec((tm, d), lambda i: (i, 0)), pl.BlockSpec((1, d), lambda i: (0, 0))],
        out_specs=pl.BlockSpec((tm, d), lambda i: (i, 0)),
        out_shape=jax.ShapeDtypeStruct((t, d), F32),
        compiler_params=_params("parallel"),
        name="final_norm",
    )(x, g.reshape(1, d))


def _regroup_w_in(w):
    o = 0
    parts = {}
    for name, size in (("q", WIDTH), ("k", WIDTH), ("v", WIDTH), ("a", HEADS), ("b", HEADS), ("g", WIDTH),
                       ("dq", WIDTH), ("dk", WIDTH), ("dv", WIDTH), ("cu", WIDTH), ("cv", WIDTH),
                       ("gates", 3 * D_MODEL)):
        parts[name] = w[..., o:o + size].astype(BF16)
        o += size
    used = _Z_HALF + _ZB_AB + 2 * HEADS
    cols = [parts[n] for n in ("gates", "g", "cu", "q", "k", "v", "dq", "dk", "dv", "cv", "a", "b")]
    cols.append(jnp.zeros(w.shape[:-1] + (2 * _Z_HALF - used,), BF16))
    return jnp.concatenate(cols, axis=-1)


def _tile(n, want):
    while n % want:
        want //= 2
    return want


def kernel(x_prompt, x_sample, cache_diff_k, cache_diff_v, page_table, cache_mem_k, cache_mem_v, state_gdn, state_gdn_conv, mem_prompt, w_in, norm_mix, gdn_conv_w, gdn_A_log, gdn_dt_bias, gdn_o_norm, diff_lq1, diff_lk1, diff_lq2, diff_lk2, diff_subln, gmlp_ln_g, gmlp_ln_b, gmlp_ws, gmlp_b, w_br_gdn, w_br_diff, w_br_gmlp, w_out, norm_xa, norm_mem, xa_wq, xa_wk, xa_wv, xa_wo, norm_ffn, ffn_w1, ffn_w2, norm_final):
    depth = w_in.shape[0]
    bp, lp, d = x_prompt.shape
    bs, ls, _ = x_sample.shape
    mem_len = mem_prompt.shape[1]
    past_len = page_table.shape[1] * PAGE
    lpad = SAMPLE_PAD
    bf = lambda a: a.astype(BF16)

    mem_tok = mem_prompt.reshape(bp * mem_len, d)
    mem_kv = [norm_matmul(mem_tok, norm_mem[l], bf(jnp.concatenate([xa_wk[l], xa_wv[l]], axis=1)),
                          _tile(bp * mem_len, 512), 512) for l in range(depth)]
    mem_k_p = jnp.stack([kv[:, :WIDTH] for kv in mem_kv]).reshape(depth, bp, mem_len, WIDTH)
    mem_v_p = jnp.stack([kv[:, WIDTH:] for kv in mem_kv]).reshape(depth, bp, mem_len, WIDTH)

    w_in_all = _regroup_w_in(w_in)
    weights = [dict(wa=bf(w_br_gdn[l]), wb=bf(w_br_diff[l]), wc=bf(w_br_gmlp[l]),
                    wo=bf(w_out[l]), xq=bf(xa_wq[l]), xo=bf(xa_wo[l]), w1=bf(ffn_w1[l]), w2=bf(ffn_w2[l]))
               for l in range(depth)]

    def trunk(x, b, l, l_valid, pos, mem_k, mem_v, s_in, buf_in, paged):
        t = b * l
        prompt = paged is None
        act = BF16 if prompt else F32
        tm = _tile(t, 1024)
        tables = _rope_tables(pos)
        outs = dict(s=[], buf=[], gv=[])
        k_all = v_all = None
        for li in range(depth):
            w = weights[li]
            lam_init = 0.8 - 0.6 * math.exp(-0.3 * li)
            lparams = (diff_lq1[li], diff_lk1[li], diff_lq2[li], diff_lk2[li])
            za, zb = proj_in(x, norm_mix[li], w_in_all, li, _tile(t, 512), 1024, act)
            za3, zb3 = za.reshape(b, l, _Z_HALF), zb.reshape(b, l, _Z_HALF)
            o_a, s_new = gdn(za3, zb3, buf_in[li], s_in, li, gdn_conv_w[li], gdn_A_log[li], gdn_dt_bias[li],
                             gdn_o_norm[li], _tile(l, 256) if prompt else GDN_CHUNK, _tile(l, 256) if prompt else l,
                             l_valid, act)
            outs["buf"].append(zb3[:, l_valid - (GDN_CONV - 1):l_valid, _ZB_QKV:_ZB_QKV + GDN_CONV_CH])
            qm, k_all, kb, v_all, vb = rope_split(zb, tables, _tile(l, 1024), act, li, depth, k_all, v_all)
            if prompt:
                o_b = flash_diff(qm, kb, vb, lparams, diff_subln[li], lam_init, b, _tile(l, 2048), _tile(l, 1024))
            else:
                ck, cv, pt = paged
                o_b = paged_diff(qm.reshape(2, b, l, WIDTH), kb.reshape(b, l, WIDTH), vb.reshape(b, l, WIDTH),
                                 ck, cv, pt, li, lparams, diff_subln[li], lam_init, l_valid)
                o_b = o_b.reshape(t, WIDTH)
            o_c, gv = gmlp(za, zb, gmlp_ln_g[li], gmlp_ln_b[li], gmlp_ws[li], gmlp_b[li], min(l, GMLP_CHUNK), act)
            x = merge(x, za, o_a.reshape(t, WIDTH), o_b, o_c, w["wa"], w["wb"], w["wc"], w["wo"], _tile(t, 512))
            if prompt:
                xa_tm = _tile(l, 1024)
                x = xattn(x, norm_xa[li], w["xq"], mem_k, mem_v, li, w["xo"], xa_tm, l // xa_tm)
            else:
                xa_seqs = 4 if b % 4 == 0 else 1
                x = xattn(x, norm_xa[li], w["xq"], mem_k, mem_v, li, w["xo"], xa_seqs * l, 1, xa_seqs)
            x = ffn(x, norm_ffn[li], w["w1"], w["w2"], tm, 1024)
            outs["s"].append(s_new)
            outs["gv"].append(gv)
        y = final_norm(x, norm_final, tm)
        seq = lambda a: a.reshape((depth, b, l) + a.shape[2:])[:, :, :l_valid]
        return (y.reshape(b, l, d)[:, :l_valid], seq(k_all), seq(v_all),
                jnp.stack(outs["s"]), jnp.stack(outs["buf"]), seq(jnp.stack(outs["gv"])))

    s0_p = jnp.zeros((depth, bp, HEADS, HW, HW), F32)
    buf0_p = jnp.zeros((depth, bp, GDN_CONV - 1, GDN_CONV_CH), F32)
    (y_p, k_p, v_p, s_p, buf_p, _) = trunk(x_prompt.reshape(bp * lp, d), bp, lp, lp, jnp.arange(lp),
                                           mem_k_p, mem_v_p, s0_p, buf0_p, None)

    x_s = jnp.pad(x_sample, ((0, 0), (0, lpad - ls), (0, 0))).reshape(bs * lpad, d)
    (y_s, k_s, v_s, s_s, buf_s, gv_s) = trunk(x_s, bs, lpad, ls, past_len + jnp.arange(lpad),
                                              cache_mem_k, cache_mem_v, state_gdn, state_gdn_conv,
                                              (cache_diff_k, cache_diff_v, page_table))

    return (y_p, y_s, k_p, v_p,
            mem_k_p.reshape(depth, bp, mem_len, HEADS, HW), mem_v_p.reshape(depth, bp, mem_len, HEADS, HW),
            s_p, buf_p, k_s, v_s, s_s, buf_s, gv_s)
```
